```python
import math
import jax, jax.numpy as jnp
from jax import lax
import numpy as np

D_MODEL = 2048
BATCH = 8
SEQ = 2048
DEPTH = 1

MIX_WIDTH = D_MODEL
ATTN_WIDTH = MIX_WIDTH // 2
HYENA_WIDTH = MIX_WIDTH - ATTN_WIDTH
N_HEADS = 8
HEAD_DIM = ATTN_WIDTH // N_HEADS // 2
V_HEAD_DIM = 2 * HEAD_DIM
Q_BLOCK = 128
SHORT_CONV = 3
FILTER_EMB = 33
FILTER_HIDDEN = 64
DECAY_FAST = 0.3
DECAY_SLOW = 1.5
DECAY_TARGET = 1e-2
DECAY_SHIFT = 0.0
D_FF = 4 * D_MODEL
ALPHA = (2.0 * DEPTH) ** 0.25
BETA = (8.0 * DEPTH) ** -0.25
EPS = 1e-5
IN_COLS = 3 * ATTN_WIDTH + 3 * HYENA_WIDTH

kernel_name = "hymba_diffattn_hyena_deepnorm_encoder"


def layer_norm(x, g, b):
    xf = x.astype(jnp.float32)
    mu = jnp.mean(xf, axis=-1, keepdims=True)
    var = jnp.mean(jnp.square(xf - mu), axis=-1, keepdims=True)
    y = (xf - mu) * lax.rsqrt(var + EPS) * g.astype(jnp.float32) + b.astype(jnp.float32)
    return y.astype(x.dtype)


def rms_norm(x, g):
    xf = x.astype(jnp.float32)
    y = xf * lax.rsqrt(jnp.mean(jnp.square(xf), axis=-1, keepdims=True) + EPS)
    return y * g.astype(jnp.float32)


def alibi_slopes(n_heads):
    return jnp.asarray(np.array([2.0 ** (-8.0 * (h + 1) / n_heads) for h in range(n_heads)], dtype=np.float32))


def diff_attention(q, k, v, lam, slopes):
    B, S = q.shape[0], q.shape[1]
    nb = S // Q_BLOCK
    scale = HEAD_DIM ** -0.5
    qb = jnp.moveaxis(q.reshape(B, nb, Q_BLOCK, N_HEADS, 2, HEAD_DIM), 1, 0)
    starts = jnp.arange(nb, dtype=jnp.int32) * Q_BLOCK
    key_pos = jnp.arange(S, dtype=jnp.int32)

    def block(args):
        q_i, start = args
        s = jnp.einsum("bqhcd,bkhcd->bhcqk", q_i, k, preferred_element_type=jnp.float32) * scale
        q_pos = start + jnp.arange(Q_BLOCK, dtype=jnp.int32)
        dist = jnp.abs(q_pos[:, None] - key_pos[None, :]).astype(jnp.float32)
        s = s - slopes[None, :, None, None, None] * dist[None, None, None]
        p = jax.nn.softmax(s, axis=-1)
        a = p[:, :, 0] - lam * p[:, :, 1]
        return jnp.einsum("bhqk,bkhe->bqhe", a.astype(v.dtype), v, preferred_element_type=jnp.float32)

    o = lax.map(block, (qb, starts))
    return jnp.moveaxis(o, 0, 1).reshape(B, S, N_HEADS, V_HEAD_DIM)


def hyena_filters(L, w1, b1, freq, w2, b2, w3):
    f32 = jnp.float32
    t = jnp.linspace(0.0, 1.0, L, dtype=f32)[:, None]
    bands = (FILTER_EMB - 1) // 2
    w = 2.0 * math.pi * jnp.arange(L, dtype=f32)[:, None] / L
    f = jnp.linspace(1e-4, bands - 1, bands, dtype=f32)[None, :]
    z = jnp.concatenate([t, jnp.cos(f * w), -jnp.sin(f * w)], axis=-1)
    fr = freq.astype(f32)
    h = jnp.sin(fr * (z @ w1.astype(f32) + b1.astype(f32)))
    h = jnp.sin(fr * (h @ w2.astype(f32) + b2.astype(f32)))
    h = h @ w3.astype(f32)
    max_decay = math.log(DECAY_TARGET) / DECAY_FAST
    min_decay = math.log(DECAY_TARGET) / DECAY_SLOW
    deltas = jnp.linspace(min_decay, max_decay, HYENA_WIDTH, dtype=f32)
    decay = jnp.exp(-t * jnp.abs(deltas)[None, :])
    h = h * (jnp.tile(decay, (1, 2)) + DECAY_SHIFT)
    return h[:, :HYENA_WIDTH], h[:, HYENA_WIDTH:]


def hyena_mixer(u, conv_w, conv_b, w1, b1, freq, w2, b2, w3, d_skip):
    L = u.shape[1]
    pad = SHORT_CONV // 2
    up = jnp.pad(u, ((0, 0), (pad, pad), (0, 0)))
    z = conv_b
    for j in range(SHORT_CONV):
        z = z + up[:, j:j + L] * conv_w[j]
    x1, x2, v = jnp.split(z, 3, axis=-1)
    hf, hb = hyena_filters(L, w1, b1, freq, w2, b2, w3)
    kern = jnp.concatenate([hf, jnp.zeros((1, HYENA_WIDTH), jnp.float32), hb[1:][::-1]], axis=0)
    n = 2 * L
    vg = (v * x2).astype(jnp.float32)
    y = jnp.fft.irfft(jnp.fft.rfft(vg, n=n, axis=1) * jnp.fft.rfft(kern, n=n, axis=0)[None], n=n, axis=1)[:, :L]
    y = y + vg * d_skip.astype(jnp.float32)
    return y * x1.astype(jnp.float32)


def setup_inputs(seed: int = 0) -> dict:
    key = jax.random.key(seed)
    ks = jax.random.split(key, 26)
    nrm = lambda k, s: jax.random.normal(k, s, dtype=jnp.float32)
    col_scale = np.ones((IN_COLS,), dtype=np.float32)
    col_scale[2 * ATTN_WIDTH:3 * ATTN_WIDTH] = BETA
    col_scale[3 * ATTN_WIDTH + 2 * HYENA_WIDTH:] = BETA
    return {
        "x": nrm(ks[0], (BATCH, SEQ, D_MODEL)),
        "w_in": nrm(ks[1], (DEPTH, D_MODEL, IN_COLS)) * (D_MODEL ** -0.5) * jnp.asarray(col_scale),
        "lambda_q1": nrm(ks[2], (DEPTH, HEAD_DIM)) * 0.1,
        "lambda_k1": nrm(ks[3], (DEPTH, HEAD_DIM)) * 0.1,
        "lambda_q2": nrm(ks[4], (DEPTH, HEAD_DIM)) * 0.1,
        "lambda_k2": nrm(ks[5], (DEPTH, HEAD_DIM)) * 0.1,
        "subln_g": 1.0 + 0.02 * nrm(ks[6], (DEPTH, V_HEAD_DIM)),
        "conv_w": nrm(ks[7], (DEPTH, SHORT_CONV, 3 * HYENA_WIDTH)) * (SHORT_CONV ** -0.5),
        "conv_b": 0.02 * nrm(ks[8], (DEPTH, 3 * HYENA_WIDTH)),
        "filt_w1": nrm(ks[9], (DEPTH, FILTER_EMB, FILTER_HIDDEN)) * (FILTER_EMB ** -0.5),
        "filt_b1": 0.1 * nrm(ks[10], (DEPTH, FILTER_HIDDEN)),
        "filt_freq": 1.0 + 0.01 * nrm(ks[11], (DEPTH, FILTER_HIDDEN)),
        "filt_w2": nrm(ks[12], (DEPTH, FILTER_HIDDEN, FILTER_HIDDEN)) * (FILTER_HIDDEN ** -0.5),
        "filt_b2": 0.1 * nrm(ks[13], (DEPTH, FILTER_HIDDEN)),
        "filt_w3": nrm(ks[14], (DEPTH, FILTER_HIDDEN, 2 * HYENA_WIDTH)) * (FILTER_HIDDEN ** -0.5) * 0.1,
        "hyena_skip": nrm(ks[15], (DEPTH, HYENA_WIDTH)),
        "hyena_gain": 1.0 + 0.02 * nrm(ks[16], (DEPTH, HYENA_WIDTH)),
        "w_out": nrm(ks[17], (DEPTH, MIX_WIDTH, D_MODEL)) * (MIX_WIDTH ** -0.5) * BETA,
        "ln1_g": 1.0 + 0.02 * nrm(ks[18], (DEPTH, D_MODEL)),
        "ln1_b": 0.02 * nrm(ks[19], (DEPTH, D_MODEL)),
        "w_ff1": nrm(ks[20], (DEPTH, D_MODEL, D_FF)) * (D_MODEL ** -0.5) * BETA,
        "w_ff2": nrm(ks[21], (DEPTH, D_FF, D_MODEL)) * (D_FF ** -0.5) * BETA,
        "ln2_g": 1.0 + 0.02 * nrm(ks[22], (DEPTH, D_MODEL)),
        "ln2_b": 0.02 * nrm(ks[23], (DEPTH, D_MODEL)),
    }


def reference(x, w_in, lambda_q1, lambda_k1, lambda_q2, lambda_k2, subln_g, conv_w, conv_b,
              filt_w1, filt_b1, filt_freq, filt_w2, filt_b2, filt_w3, hyena_skip, hyena_gain,
              w_out, ln1_g, ln1_b, w_ff1, w_ff2, ln2_g, ln2_b):
    B, S, _ = x.shape
    A = ATTN_WIDTH
    slopes = alibi_slopes(N_HEADS)
    for l in range(DEPTH):
        lam_init = 0.8 - 0.6 * math.exp(-0.3 * l)
        proj = jnp.einsum("bsd,dn->bsn", x, w_in[l])
        q = proj[..., :A].reshape(B, S, N_HEADS, 2, HEAD_DIM)
        k = proj[..., A:2 * A].reshape(B, S, N_HEADS, 2, HEAD_DIM)
        v = proj[..., 2 * A:3 * A].reshape(B, S, N_HEADS, V_HEAD_DIM)
        lam = (jnp.exp(jnp.sum(lambda_q1[l].astype(jnp.float32) * lambda_k1[l].astype(jnp.float32)))
               - jnp.exp(jnp.sum(lambda_q2[l].astype(jnp.float32) * lambda_k2[l].astype(jnp.float32)))
               + lam_init)
        att = diff_attention(q, k, v, lam, slopes)
        att = (rms_norm(att, subln_g[l]) * (1.0 - lam_init)).reshape(B, S, A).astype(x.dtype)
        hy = hyena_mixer(proj[..., 3 * A:], conv_w[l], conv_b[l], filt_w1[l], filt_b1[l], filt_freq[l],
                         filt_w2[l], filt_b2[l], filt_w3[l], hyena_skip[l])
        hy = rms_norm(hy, hyena_gain[l]).astype(x.dtype)
        mix = jnp.einsum("bsm,md->bsd", jnp.concatenate([att, hy], axis=-1), w_out[l])
        x = layer_norm(ALPHA * x + mix, ln1_g[l], ln1_b[l])
        h = jnp.square(jax.nn.relu(jnp.einsum("bsd,df->bsf", x, w_ff1[l])))
        x = layer_norm(ALPHA * x + jnp.einsum("bsf,fd->bsd", h, w_ff2[l]), ln2_g[l], ln2_b[l])
    return x
```

```python
import functools
import math

import numpy as np
import jax
import jax.numpy as jnp
from jax import lax
from jax.experimental import pallas as pl
from jax.experimental.pallas import tpu as pltpu

D_MODEL = 2048
BATCH = 8
SEQ = 2048
DEPTH = 1
ATTN_WIDTH = D_MODEL // 2
HYENA_WIDTH = D_MODEL - ATTN_WIDTH
N_HEADS = 8
HEAD_DIM = ATTN_WIDTH // N_HEADS // 2
V_HEAD_DIM = 2 * HEAD_DIM
SHORT_CONV = 3
FILTER_EMB = 33
FILTER_HIDDEN = 64
DECAY_FAST = 0.3
DECAY_SLOW = 1.5
DECAY_TARGET = 1e-2
DECAY_SHIFT = 0.0
D_FF = 4 * D_MODEL
ALPHA = (2.0 * DEPTH) ** 0.25
EPS = 1e-5
IN_COLS = 3 * ATTN_WIDTH + 3 * HYENA_WIDTH
FFT_N = 2 * SEQ
LANE = 128
DFT_TILE = 512

BF16 = jnp.bfloat16
F32 = jnp.float32

_MIB = 1024 * 1024


def _params(semantics, vmem_mib):
    return pltpu.CompilerParams(dimension_semantics=semantics, vmem_limit_bytes=vmem_mib * _MIB)


@functools.lru_cache(maxsize=None)
def _dft_table():
    k = np.arange(SEQ, dtype=np.int64)
    phase = (k[:, None] * k[None, :]) % FFT_N
    ang = (2.0 * np.pi / FFT_N) * phase
    c = np.cos(ang)
    s = -np.sin(ang)
    s[0, :] = np.where(k % 2 == 0, 1.0, -1.0)
    nt = SEQ // DFT_TILE
    tiled = np.concatenate([c.reshape(nt, DFT_TILE, SEQ), s.reshape(nt, DFT_TILE, SEQ)], axis=1)
    return tiled.astype(np.float32)


@functools.lru_cache(maxsize=None)
def _filter_tables():
    L = SEQ
    t = np.linspace(0.0, 1.0, L)[:, None]
    bands = (FILTER_EMB - 1) // 2
    w = 2.0 * np.pi * np.arange(L, dtype=np.float64)[:, None] / L
    f = np.linspace(1e-4, bands - 1, bands)[None, :]
    z = np.concatenate([t, np.cos(f * w), -np.sin(f * w)], axis=-1)
    zpad = np.zeros((L, LANE), np.float32)
    zpad[:, :FILTER_EMB] = z
    max_decay = math.log(DECAY_TARGET) / DECAY_FAST
    min_decay = math.log(DECAY_TARGET) / DECAY_SLOW
    deltas = np.abs(np.linspace(min_decay, max_decay, HYENA_WIDTH))[None, :].astype(np.float32)
    return zpad, deltas


def _inproj_kernel(x_ref, w_ref, o_ref, xb_ref):
    @pl.when(pl.program_id(1) == 0)
    def _():
        xb_ref[...] = x_ref[...].astype(BF16)

    o_ref[...] = jnp.dot(xb_ref[...], w_ref[...], preferred_element_type=F32).astype(BF16)


def _inproj(x2d, w_bf16, tm=1024, tn=1024):
    m, k = x2d.shape
    n = w_bf16.shape[1]
    return pl.pallas_call(
        _inproj_kernel,
        grid=(m // tm, n // tn),
        in_specs=[pl.BlockSpec((tm, k), lambda i, j: (i, 0)),
                  pl.BlockSpec((k, tn), lambda i, j: (0, j))],
        out_specs=pl.BlockSpec((tm, tn), lambda i, j: (i, j)),
        out_shape=jax.ShapeDtypeStruct((m, n), BF16),
        scratch_shapes=[pltpu.VMEM((tm, k), BF16)],
        compiler_params=_params(("parallel", "arbitrary"), 48),
        name="inproj",
    )(x2d, w_bf16)


def _attn_kernel(lam_ref, slopes_ref, q_ref, k_ref, v_ref, g_ref, o_ref, *, tq, lam_init):
    h = pl.program_id(1)
    qb = pl.program_id(2)
    lam = lam_ref[0]
    slope = slopes_ref[h]
    q = q_ref[0]
    k = k_ref[0]
    v = v_ref[0]
    s_len = k.shape[0]
    lane = lax.broadcasted_iota(jnp.int32, (1, 2 * HEAD_DIM), 1)
    qpos = qb * tq + lax.broadcasted_iota(jnp.int32, (tq, 1), 0)
    kpos = lax.broadcasted_iota(jnp.int32, (1, s_len), 1)
    bias = slope * jnp.abs(qpos - kpos).astype(F32)
    scale = HEAD_DIM ** -0.5
    outs = []
    for c in range(2):
        sel = (lane < HEAD_DIM) if c == 0 else (lane >= HEAD_DIM)
        qc = jnp.where(sel, q, jnp.zeros_like(q))
        s = lax.dot_general(qc, k, (((1,), (1,)), ((), ())), preferred_element_type=F32)
        s = s * scale - bias
        m = jnp.max(s, axis=-1, keepdims=True)
        p = jnp.exp(s - m)
        l = jnp.sum(p, axis=-1, keepdims=True)
        o = jnp.dot(p.astype(BF16), v, preferred_element_type=F32)
        outs.append(o / l)
    a = outs[0] - lam * outs[1]
    ms = jnp.mean(a * a, axis=-1, keepdims=True)
    y = a * lax.rsqrt(ms + EPS) * g_ref[...] * (1.0 - lam_init)
    o_ref[0] = y.astype(BF16)


def _attention(proj3, lam, slopes, subln_g, lam_init, tq=256):
    b, s, _ = proj3.shape
    nh = N_HEADS
    kern = functools.partial(_attn_kernel, tq=tq, lam_init=lam_init)
    smem = pl.BlockSpec(memory_space=pltpu.SMEM)
    return pl.pallas_call(
        kern,
        grid=(b, nh, s // tq),
        in_specs=[smem, smem,
                  pl.BlockSpec((1, tq, V_HEAD_DIM), lambda bi, hi, qi: (bi, qi, hi)),
                  pl.BlockSpec((1, s, V_HEAD_DIM), lambda bi, hi, qi: (bi, 0, nh + hi)),
                  pl.BlockSpec((1, s, V_HEAD_DIM), lambda bi, hi, qi: (bi, 0, 2 * nh + hi)),
                  pl.BlockSpec((1, V_HEAD_DIM), lambda bi, hi, qi: (0, 0))],
        out_specs=pl.BlockSpec((1, tq, V_HEAD_DIM), lambda bi, hi, qi: (bi, qi, hi)),
        out_shape=jax.ShapeDtypeStruct((b, s, ATTN_WIDTH), BF16),
        compiler_params=_params(("parallel", "parallel", "arbitrary"), 48),
        name="diff_attn",
    )(lam, slopes, proj3, proj3, proj3, subln_g)


def _gate_kernel(u1_ref, u2_ref, u3_ref, w1_ref, w2_ref, w3_ref, b1_ref, b2_ref, b3_ref, vg_ref, x1_ref):
    s_len = u1_ref.shape[1]
    row = lax.broadcasted_iota(jnp.int32, (s_len, 1), 0)

    def conv(u_ref, w_ref, b_ref):
        u = u_ref[0].astype(F32)
        prev = jnp.where(row == 0, 0.0, pltpu.roll(u, 1, 0))
        nxt = jnp.where(row == s_len - 1, 0.0, pltpu.roll(u, s_len - 1, 0))
        w = w_ref[...]
        return b_ref[...] + prev * w[0:1] + u * w[1:2] + nxt * w[2:3]

    x1 = conv(u1_ref, w1_ref, b1_ref)
    x2 = conv(u2_ref, w2_ref, b2_ref)
    v = conv(u3_ref, w3_ref, b3_ref)
    vg_ref[0] = (v * x2).astype(BF16)
    x1_ref[0] = x1.astype(BF16)


def _gate(proj3, conv_w, conv_b, tc=256):
    b, s, _ = proj3.shape
    c = HYENA_WIDTH
    nb = c // tc
    base = 3 * ATTN_WIDTH // tc

    def uspec(part):
        return pl.BlockSpec((1, s, tc), lambda bi, ci: (bi, 0, base + part * nb + ci))

    def wspec(part):
        return pl.BlockSpec((SHORT_CONV, tc), lambda bi, ci: (0, part * nb + ci))

    def bspec(part):
        return pl.BlockSpec((1, tc), lambda bi, ci: (0, part * nb + ci))

    ospec = pl.BlockSpec((1, s, tc), lambda bi, ci: (bi, 0, ci))
    return pl.pallas_call(
        _gate_kernel,
        grid=(b, nb),
        in_specs=[uspec(0), uspec(1), uspec(2), wspec(0), wspec(1), wspec(2), bspec(0), bspec(1), bspec(2)],
        out_specs=[ospec, ospec],
        out_shape=[jax.ShapeDtypeStruct((b, s, c), BF16), jax.ShapeDtypeStruct((b, s, c), BF16)],
        compiler_params=_params(("parallel", "parallel"), 48),
        name="hyena_gate",
    )(proj3, proj3, proj3, conv_w, conv_w, conv_w, conv_b, conv_b, conv_b)


def _filter_kernel(z_ref, w1_ref, b1_ref, fr_ref, w2_ref, b2_ref, w3_ref, dl_ref, o_ref, *, tr):
    hi = lax.Precision.HIGHEST
    fr = fr_ref[...]
    h = jnp.sin(fr * (jnp.dot(z_ref[...], w1_ref[...], precision=hi, preferred_element_type=F32) + b1_ref[...]))
    h = jnp.sin(fr * (jnp.dot(h, w2_ref[...], precision=hi, preferred_element_type=F32) + b2_ref[...]))
    h = jnp.dot(h, w3_ref[...], precision=hi, preferred_element_type=F32)
    row = pl.program_id(0) * tr + lax.broadcasted_iota(jnp.int32, (tr, 1), 0)
    t = row.astype(F32) * (1.0 / (SEQ - 1))
    decay = jnp.exp(-t * dl_ref[...]) + DECAY_SHIFT
    c = HYENA_WIDTH
    o_ref[:, :c] = (h[:, :c] * decay).astype(BF16)
    o_ref[:, c:] = jnp.where(row == 0, 0.0, h[:, c:] * decay).astype(BF16)


def _filters(zpad, w1pad, b1, freq, w2, b2, w3, deltas, tr=256):
    L = SEQ
    c2 = 2 * HYENA_WIDTH
    full = lambda shape: pl.BlockSpec(shape, lambda i: (0, 0))
    return pl.pallas_call(
        functools.partial(_filter_kernel, tr=tr),
        grid=(L // tr,),
        in_specs=[pl.BlockSpec((tr, LANE), lambda i: (i, 0)),
                  full((LANE, FILTER_HIDDEN)), full((1, FILTER_HIDDEN)), full((1, FILTER_HIDDEN)),
                  full((FILTER_HIDDEN, FILTER_HIDDEN)), full((1, FILTER_HIDDEN)),
                  full((FILTER_HIDDEN, c2)), full((1, HYENA_WIDTH))],
        out_specs=pl.BlockSpec((tr, c2), lambda i: (i, 0)),
        out_shape=jax.ShapeDtypeStruct((L, c2), BF16),
        compiler_params=_params(("parallel",), 48),
        name="hyena_filters",
    )(zpad, w1pad, b1, freq, w2, b2, w3, deltas)


def _kspec_kernel(f_ref, h_ref, o_ref):
    c = HYENA_WIDTH
    t = DFT_TILE
    p = jnp.dot(f_ref[0].astype(BF16), h_ref[...], preferred_element_type=F32)
    row = pl.program_id(0) * t + lax.broadcasted_iota(jnp.int32, (t, 1), 0)
    o_ref[0] = p[:t, :c] + p[:t, c:]
    o_ref[1] = jnp.where(row == 0, p[t:, :c] + p[t:, c:], p[t:, :c] - p[t:, c:])


def _kspec(table, hcat):
    L = SEQ
    c = HYENA_WIDTH
    t = DFT_TILE
    return pl.pallas_call(
        _kspec_kernel,
        grid=(L // t,),
        in_specs=[pl.BlockSpec((1, 2 * t, L), lambda i: (i, 0, 0)),
                  pl.BlockSpec((L, 2 * c), lambda i: (0, 0), pipeline_mode=pl.Buffered(1))],
        out_specs=pl.BlockSpec((2, t, c), lambda i: (0, i, 0)),
        out_shape=jax.ShapeDtypeStruct((2, L, c), F32),
        compiler_params=_params(("arbitrary",), 56),
        name="hyena_kspec",
    )(table, hcat)


def _fwd_kernel(f_ref, vg_ref, k_ref, y_ref, fb_ref):
    t = DFT_TILE

    @pl.when(pl.program_id(1) == 0)
    def _():
        fb_ref[...] = f_ref[0].astype(BF16)

    x = jnp.dot(fb_ref[...], vg_ref[0], preferred_element_type=F32)
    xr = x[:t]
    xi = x[t:]
    kr = k_ref[0]
    ki = k_ref[1]
    row = pl.program_id(0) * t + lax.broadcasted_iota(jnp.int32, (t, 1), 0)
    first = row == 0
    yr = jnp.where(first, xr * kr, xr * kr - xi * ki)
    yi = jnp.where(first, xi * ki, xr * ki + xi * kr)
    y_ref[0, 0] = yr.astype(BF16)
    y_ref[0, 1] = yi.astype(BF16)


def _fwd_dft(table, vg, kspec):
    b, L, c = vg.shape
    t = DFT_TILE
    return pl.pallas_call(
        _fwd_kernel,
        grid=(L // t, b),
        in_specs=[pl.BlockSpec((1, 2 * t, L), lambda j, bi: (j, 0, 0), pipeline_mode=pl.Buffered(1)),
                  pl.BlockSpec((1, L, c), lambda j, bi: (bi, 0, 0)),
                  pl.BlockSpec((2, t, c), lambda j, bi: (0, j, 0))],
        out_specs=pl.BlockSpec((1, 2, t, c), lambda j, bi: (bi, 0, j, 0)),
        out_shape=jax.ShapeDtypeStruct((b, 2, L, c), BF16),
        scratch_shapes=[pltpu.VMEM((2 * t, L), BF16)],
        compiler_params=_params(("arbitrary", "arbitrary"), 56),
        name="hyena_fwd_dft",
    )(table, vg, kspec)


def _inv_kernel(f_ref, y_ref, vg_ref, x1_ref, d_ref, gain_ref, o_ref, g_ref):
    t = DFT_TILE
    L = SEQ

    @pl.when(pl.program_id(1) == 0)
    def _():
        row = pl.program_id(0) * t + lax.broadcasted_iota(jnp.int32, (t, 1), 0)
        col = lax.broadcasted_iota(jnp.int32, (1, L), 1)
        sign = (1 - 2 * (row & 1)).astype(F32)
        gc = f_ref[0, :t, :] * jnp.where(col == 0, 1.0 / FFT_N, 2.0 / FFT_N)
        gs = jnp.where(col == 0, sign * (1.0 / FFT_N),
                       jnp.where(row == 0, 0.0, f_ref[0, t:, :] * (2.0 / FFT_N)))
        g_ref[:, :L] = gc.astype(BF16)
        g_ref[:, L:] = gs.astype(BF16)

    y = jnp.dot(g_ref[...], y_ref[0], preferred_element_type=F32)
    vg = vg_ref[0].astype(F32)
    y = (y + vg * d_ref[...]) * x1_ref[0].astype(F32)
    ms = jnp.mean(y * y, axis=-1, keepdims=True)
    o_ref[0] = (y * lax.rsqrt(ms + EPS) * gain_ref[...]).astype(BF16)


def _inv_dft(table, yspec, vg, x1, d_skip, gain):
    b, L, c = vg.shape
    t = DFT_TILE
    return pl.pallas_call(
        _inv_kernel,
        grid=(L // t, b),
        in_specs=[pl.BlockSpec((1, 2 * t, L), lambda j, bi: (j, 0, 0), pipeline_mode=pl.Buffered(1)),
                  pl.BlockSpec((1, 2 * L, c), lambda j, bi: (bi, 0, 0)),
                  pl.BlockSpec((1, t, c), lambda j, bi: (bi, j, 0)),
                  pl.BlockSpec((1, t, c), lambda j, bi: (bi, j, 0)),
                  pl.BlockSpec((1, c), lambda j, bi: (0, 0)),
                  pl.BlockSpec((1, c), lambda j, bi: (0, 0))],
        out_specs=pl.BlockSpec((1, t, c), lambda j, bi: (bi, j, 0)),
        out_shape=jax.ShapeDtypeStruct((b, L, c), BF16),
        scratch_shapes=[pltpu.VMEM((t, 2 * L), BF16)],
        compiler_params=_params(("arbitrary", "arbitrary"), 56),
        name="hyena_inv_dft",
    )(table, yspec, vg, x1, d_skip, gain)


def _layer_norm(y, g, b):
    mu = jnp.mean(y, axis=-1, keepdims=True)
    yc = y - mu
    var = jnp.mean(yc * yc, axis=-1, keepdims=True)
    return yc * lax.rsqrt(var + EPS) * g + b


def _outproj_kernel(att_ref, hy_ref, w_ref, x_ref, g_ref, b_ref, o_ref):
    a = ATTN_WIDTH
    mix = jnp.dot(att_ref[...], w_ref[:a, :], preferred_element_type=F32)
    mix = mix + jnp.dot(hy_ref[...], w_ref[a:, :], preferred_element_type=F32)
    y = ALPHA * x_ref[...] + mix
    o_ref[...] = _layer_norm(y, g_ref[...], b_ref[...])


def _outproj(att2d, hy2d, w_bf16, x2d, g, b, tm=512):
    m, d = x2d.shape
    return pl.pallas_call(
        _outproj_kernel,
        grid=(m // tm,),
        in_specs=[pl.BlockSpec((tm, ATTN_WIDTH), lambda i: (i, 0)),
                  pl.BlockSpec((tm, HYENA_WIDTH), lambda i: (i, 0)),
                  pl.BlockSpec((d, d), lambda i: (0, 0)),
                  pl.BlockSpec((tm, d), lambda i: (i, 0)),
                  pl.BlockSpec((1, d), lambda i: (0, 0)),
                  pl.BlockSpec((1, d), lambda i: (0, 0))],
        out_specs=pl.BlockSpec((tm, d), lambda i: (i, 0)),
        out_shape=jax.ShapeDtypeStruct((m, d), F32),
        compiler_params=_params(("parallel",), 56),
        name="outproj_ln",
    )(att2d, hy2d, w_bf16, x2d, g, b)


def _ffn_kernel(x_ref, w1_ref, w2_ref, g_ref, b_ref, o_ref, xb_ref, acc_ref):
    f = pl.program_id(1)

    @pl.when(f == 0)
    def _():
        xb_ref[...] = x_ref[...].astype(BF16)
        acc_ref[...] = jnp.zeros_like(acc_ref)

    h = jnp.dot(xb_ref[...], w1_ref[...], preferred_element_type=F32)
    h = jnp.square(jnp.maximum(h, 0.0)).astype(BF16)
    acc_ref[...] += jnp.dot(h, w2_ref[...], preferred_element_type=F32)

    @pl.when(f == pl.num_programs(1) - 1)
    def _():
        y = ALPHA * x_ref[...] + acc_ref[...]
        o_ref[...] = _layer_norm(y, g_ref[...], b_ref[...])


def _ffn(x2d, w1_bf16, w2_bf16, g, b, tm=512, tf=512):
    m, d = x2d.shape
    dff = w1_bf16.shape[1]
    return pl.pallas_call(
        _ffn_kernel,
        grid=(m // tm, dff // tf),
        in_specs=[pl.BlockSpec((tm, d), lambda i, j: (i, 0)),
                  pl.BlockSpec((d, tf), lambda i, j: (0, j)),
                  pl.BlockSpec((tf, d), lambda i, j: (j, 0)),
                  pl.BlockSpec((1, d), lambda i, j: (0, 0)),
                  pl.BlockSpec((1, d), lambda i, j: (0, 0))],
        out_specs=pl.BlockSpec((tm, d), lambda i, j: (i, 0)),
        out_shape=jax.ShapeDtypeStruct((m, d), F32),
        scratch_shapes=[pltpu.VMEM((tm, d), BF16), pltpu.VMEM((tm, d), F32)],
        compiler_params=_params(("parallel", "arbitrary"), 56),
        name="ffn_ln",
    )(x2d, w1_bf16, w2_bf16, g, b)


def kernel(x, w_in, lambda_q1, lambda_k1, lambda_q2, lambda_k2, subln_g, conv_w, conv_b, filt_w1, filt_b1,
           filt_freq, filt_w2, filt_b2, filt_w3, hyena_skip, hyena_gain, w_out, ln1_g, ln1_b, w_ff1, w_ff2,
           ln2_g, ln2_b):
    B, S, D = x.shape
    assert (B, S, D) == (BATCH, SEQ, D_MODEL)
    zpad_np, deltas_np = _filter_tables()
    dft_tab = jnp.asarray(_dft_table())
    zpad = jnp.asarray(zpad_np)
    deltas = jnp.asarray(deltas_np)
    slopes = jnp.asarray(np.array([2.0 ** (-8.0 * (h + 1) / N_HEADS) for h in range(N_HEADS)], dtype=np.float32))
    row = lambda v: v.astype(F32).reshape(1, -1)

    x2d = x.reshape(B * S, D)
    for l in range(DEPTH):
        lam_init = 0.8 - 0.6 * math.exp(-0.3 * l)
        lam = (jnp.exp(jnp.sum(lambda_q1[l].astype(F32) * lambda_k1[l].astype(F32)))
               - jnp.exp(jnp.sum(lambda_q2[l].astype(F32) * lambda_k2[l].astype(F32)))
               + lam_init).reshape(1)

        proj = _inproj(x2d, w_in[l].astype(BF16))
        proj3 = proj.reshape(B, S, IN_COLS)

        att = _attention(proj3, lam, slopes, row(subln_g[l]), lam_init)

        vg, x1 = _gate(proj3, conv_w[l].astype(F32), row(conv_b[l]))
        w1pad = jnp.zeros((LANE, FILTER_HIDDEN), F32).at[:FILTER_EMB].set(filt_w1[l].astype(F32))
        hcat = _filters(zpad, w1pad, row(filt_b1[l]), row(filt_freq[l]), filt_w2[l].astype(F32),
                        row(filt_b2[l]), filt_w3[l].astype(F32), deltas)
        kspec = _kspec(dft_tab, hcat)
        yspec = _fwd_dft(dft_tab, vg, kspec)
        hy = _inv_dft(dft_tab, yspec.reshape(B, 2 * S, HYENA_WIDTH), vg, x1, row(hyena_skip[l]),
                      row(hyena_gain[l]))

        x2d = _outproj(att.reshape(B * S, ATTN_WIDTH), hy.reshape(B * S, HYENA_WIDTH), w_out[l].astype(BF16),
                       x2d, row(ln1_g[l]), row(ln1_b[l]))
        x2d = _ffn(x2d, w_ff1[l].astype(BF16), w_ff2[l].astype(BF16), row(ln2_g[l]), row(ln2_b[l]))
    return x2d.reshape(B, S, D)
```

```python
import functools
import math

import numpy as np
import jax
import jax.numpy as jnp
from jax import lax
from jax.experimental import pallas as pl
from jax.experimental.pallas import tpu as pltpu

D_MODEL = 2048
BATCH = 8
SEQ = 2048
DEPTH = 1
ATTN_WIDTH = D_MODEL // 2
HYENA_WIDTH = D_MODEL - ATTN_WIDTH
N_HEADS = 8
HEAD_DIM = ATTN_WIDTH // N_HEADS // 2
V_HEAD_DIM = 2 * HEAD_DIM
SHORT_CONV = 3
FILTER_EMB = 33
FILTER_HIDDEN = 64
DECAY_FAST = 0.3
DECAY_SLOW = 1.5
DECAY_TARGET = 1e-2
DECAY_SHIFT = 0.0
D_FF = 4 * D_MODEL
ALPHA = (2.0 * DEPTH) ** 0.25
EPS = 1e-5
IN_COLS = 3 * ATTN_WIDTH + 3 * HYENA_WIDTH
FFT_N = 2 * SEQ
LANE = 128
DFT_TILE = 512
ATTN_TILE = 512
LOG2E = math.log2(math.e)
Q_PRESCALE = HEAD_DIM ** -0.5 * LOG2E

BF16 = jnp.bfloat16
F32 = jnp.float32

_MIB = 1024 * 1024


def _params(semantics, vmem_mib):
    return pltpu.CompilerParams(dimension_semantics=semantics, vmem_limit_bytes=vmem_mib * _MIB)


@functools.lru_cache(maxsize=None)
def _dft_table():
    k = np.arange(SEQ, dtype=np.int64)
    phase = (k[:, None] * k[None, :]) % FFT_N
    ang = (2.0 * np.pi / FFT_N) * phase
    c = np.cos(ang)
    s = -np.sin(ang)
    s[0, :] = np.where(k % 2 == 0, 1.0, -1.0)
    nt = SEQ // DFT_TILE
    tiled = np.concatenate([c.reshape(nt, DFT_TILE, SEQ), s.reshape(nt, DFT_TILE, SEQ)], axis=1)
    return tiled.astype(np.float32)


@functools.lru_cache(maxsize=None)
def _filter_tables():
    L = SEQ
    t = np.linspace(0.0, 1.0, L)[:, None]
    bands = (FILTER_EMB - 1) // 2
    w = 2.0 * np.pi * np.arange(L, dtype=np.float64)[:, None] / L
    f = np.linspace(1e-4, bands - 1, bands)[None, :]
    z = np.concatenate([t, np.cos(f * w), -np.sin(f * w)], axis=-1)
    zpad = np.zeros((L, LANE), np.float32)
    zpad[:, :FILTER_EMB] = z
    max_decay = math.log(DECAY_TARGET) / DECAY_FAST
    min_decay = math.log(DECAY_TARGET) / DECAY_SLOW
    deltas = np.abs(np.linspace(min_decay, max_decay, HYENA_WIDTH))[None, :].astype(np.float32)
    return zpad, deltas


def _inproj_kernel(x_ref, w_ref, o_ref, xb_ref):
    @pl.when(pl.program_id(1) == 0)
    def _():
        xb_ref[...] = x_ref[...].astype(BF16)

    scale = jnp.where(pl.program_id(1) == 0, Q_PRESCALE, 1.0)
    acc = jnp.dot(xb_ref[...], w_ref[...], preferred_element_type=F32)
    o_ref[...] = (acc * scale).astype(BF16)


def _inproj(x2d, w_bf16, tm=1024, tn=ATTN_WIDTH):
    m, k = x2d.shape
    n = w_bf16.shape[1]
    return pl.pallas_call(
        _inproj_kernel,
        grid=(m // tm, n // tn),
        in_specs=[pl.BlockSpec((tm, k), lambda i, j: (i, 0)),
                  pl.BlockSpec((k, tn), lambda i, j: (0, j))],
        out_specs=pl.BlockSpec((tm, tn), lambda i, j: (i, j)),
        out_shape=jax.ShapeDtypeStruct((m, n), BF16),
        scratch_shapes=[pltpu.VMEM((tm, k), BF16)],
        compiler_params=_params(("parallel", "arbitrary"), 48),
        name="inproj",
    )(x2d, w_bf16)


@functools.lru_cache(maxsize=None)
def _alibi_tables():
    pos = np.arange(SEQ)
    hi = ((pos >> 8) << 8).astype(np.float64)
    lo = (pos & 255).astype(np.float64)
    augq = np.zeros((N_HEADS, SEQ, V_HEAD_DIM), np.float64)
    augk = np.zeros((N_HEADS, SEQ, V_HEAD_DIM), np.float64)
    t = ATTN_TILE
    ahead = np.maximum(np.arange(t)[None, :] - np.arange(t)[:, None], 0).astype(np.float64)
    dg = np.zeros((N_HEADS, t, t), np.float64)
    for h in range(N_HEADS):
        c = 2.0 ** (-8.0 * (h + 1) / N_HEADS) * LOG2E
        rest = c
        for p in range(3):
            piece = float(np.float32(rest).astype(BF16).astype(np.float64))
            rest -= piece
            for base in (0, HEAD_DIM):
                augq[h, :, base + 2 * p] = piece
                augq[h, :, base + 2 * p + 1] = piece
                augk[h, :, base + 2 * p] = hi
                augk[h, :, base + 2 * p + 1] = lo
                augq[h, :, base + 6 + 2 * p] = -hi
                augq[h, :, base + 6 + 2 * p + 1] = -lo
                augk[h, :, base + 6 + 2 * p] = piece
                augk[h, :, base + 6 + 2 * p + 1] = piece
        dg[h] = -2.0 * c * ahead
    return augq.astype(BF16), augk.astype(BF16), dg.astype(np.float32)


def _attn_kernel(lam_ref, q_ref, k_ref, v_ref, aq_ref, ak_ref, dg_ref, g_ref, o_ref, kt_ref, s_ref, *, lam_init):
    t = ATTN_TILE
    nkb = SEQ // t
    qb = pl.program_id(2)
    lam = lam_ref[0]
    lane = lax.broadcasted_iota(jnp.int32, (1, V_HEAD_DIM), 1)
    first_half = lane < HEAD_DIM
    nt_dims = (((1,), (1,)), ((), ()))

    @pl.when(qb == 0)
    def _():
        k = k_ref[0]
        ak = ak_ref[0]
        kt_ref[0] = jnp.where(first_half, k, ak)
        kt_ref[1] = jnp.where(first_half, ak, k)

    q = q_ref[0]
    aq = aq_ref[0]
    naq = -aq
    halves = (first_half, jnp.logical_not(first_half))
    q_left = [jnp.where(keep, q, aq) for keep in halves]
    q_right = [jnp.where(keep, q, naq) for keep in halves]

    def key_start(r):
        kb = qb + r
        kb = jnp.where(kb >= nkb, kb - nkb, kb)
        return pl.multiple_of(kb * t, t)

    def scores(c, r):
        lhs = q_left[c] if r == 0 else jnp.where(qb + r >= nkb, q_left[c], q_right[c])
        s = lax.dot_general(lhs, kt_ref[c, pl.ds(key_start(r), t), :], nt_dims, preferred_element_type=F32)
        if r == 0:
            s = s + dg_ref[0]
        s_ref[c, r] = s
        return jnp.max(s, axis=-1, keepdims=True)

    def probs(c, r, m, l, acc):
        p = jnp.exp2(s_ref[c, r] - m)
        l = l + jnp.sum(p, axis=-1, keepdims=True)
        acc = acc + jnp.dot(p.astype(BF16), v_ref[0, pl.ds(key_start(r), t), :], preferred_element_type=F32)
        return l, acc

    zero_l = jnp.zeros((t, 1), F32)
    zero_acc = jnp.zeros((t, V_HEAD_DIM), F32)
    m0 = scores(0, 0)
    for r in range(1, nkb):
        m0 = jnp.maximum(m0, scores(0, r))
    l0, acc0 = probs(0, 0, m0, zero_l, zero_acc)
    m1 = scores(1, 0)
    for r in range(1, nkb):
        l0, acc0 = probs(0, r, m0, l0, acc0)
        m1 = jnp.maximum(m1, scores(1, r))
    l1, acc1 = zero_l, zero_acc
    for r in range(nkb):
        l1, acc1 = probs(1, r, m1, l1, acc1)
    outs = [acc0 / l0, acc1 / l1]
    a = outs[0] - lam * outs[1]
    ms = jnp.mean(a * a, axis=-1, keepdims=True)
    y = a * lax.rsqrt(ms + EPS) * g_ref[...] * (1.0 - lam_init)
    o_ref[0] = y.astype(BF16)


def _attention(proj3, lam, subln_g, lam_init):
    b, s, _ = proj3.shape
    nh = N_HEADS
    t = ATTN_TILE
    augq_np, augk_np, dg_np = _alibi_tables()
    kern = functools.partial(_attn_kernel, lam_init=lam_init)
    smem = pl.BlockSpec(memory_space=pltpu.SMEM)
    return pl.pallas_call(
        kern,
        grid=(nh, b, s // t),
        in_specs=[smem,
                  pl.BlockSpec((1, t, V_HEAD_DIM), lambda hi, bi, qi: (bi, qi, hi)),
                  pl.BlockSpec((1, s, V_HEAD_DIM), lambda hi, bi, qi: (bi, 0, nh + hi)),
                  pl.BlockSpec((1, s, V_HEAD_DIM), lambda hi, bi, qi: (bi, 0, 2 * nh + hi)),
                  pl.BlockSpec((1, t, V_HEAD_DIM), lambda hi, bi, qi: (hi, qi, 0)),
                  pl.BlockSpec((1, s, V_HEAD_DIM), lambda hi, bi, qi: (hi, 0, 0)),
                  pl.BlockSpec((1, t, t), lambda hi, bi, qi: (hi, 0, 0)),
                  pl.BlockSpec((1, V_HEAD_DIM), lambda hi, bi, qi: (0, 0))],
        out_specs=pl.BlockSpec((1, t, V_HEAD_DIM), lambda hi, bi, qi: (bi, qi, hi)),
        out_shape=jax.ShapeDtypeStruct((b, s, ATTN_WIDTH), BF16),
        scratch_shapes=[pltpu.VMEM((2, s, V_HEAD_DIM), BF16), pltpu.VMEM((2, s // t, t, t), F32)],
        compiler_params=_params(("arbitrary", "arbitrary", "arbitrary"), 48),
        name="diff_attn",
    )(lam, proj3, proj3, proj3, jnp.asarray(augq_np), jnp.asarray(augk_np), jnp.asarray(dg_np), subln_g)


def _gate_kernel(u1_ref, u2_ref, u3_ref, w1_ref, w2_ref, w3_ref, b1_ref, b2_ref, b3_ref, vg_ref, x1_ref):
    s_len = u1_ref.shape[1]
    row = lax.broadcasted_iota(jnp.int32, (s_len, 1), 0)

    def conv(u_ref, w_ref, b_ref):
        u = u_ref[0].astype(F32)
        prev = jnp.where(row == 0, 0.0, pltpu.roll(u, 1, 0))
        nxt = jnp.where(row == s_len - 1, 0.0, pltpu.roll(u, s_len - 1, 0))
        w = w_ref[...]
        return b_ref[...] + prev * w[0:1] + u * w[1:2] + nxt * w[2:3]

    x1 = conv(u1_ref, w1_ref, b1_ref)
    x2 = conv(u2_ref, w2_ref, b2_ref)
    v = conv(u3_ref, w3_ref, b3_ref)
    vg_ref[0] = (v * x2).astype(BF16)
    x1_ref[0] = x1.astype(BF16)


def _gate(proj3, conv_w, conv_b, tc=256):
    b, s, _ = proj3.shape
    c = HYENA_WIDTH
    nb = c // tc
    base = 3 * ATTN_WIDTH // tc

    def uspec(part):
        return pl.BlockSpec((1, s, tc), lambda bi, ci: (bi, 0, base + part * nb + ci))

    def wspec(part):
        return pl.BlockSpec((SHORT_CONV, tc), lambda bi, ci: (0, part * nb + ci))

    def bspec(part):
        return pl.BlockSpec((1, tc), lambda bi, ci: (0, part * nb + ci))

    ospec = pl.BlockSpec((1, s, tc), lambda bi, ci: (bi, 0, ci))
    return pl.pallas_call(
        _gate_kernel,
        grid=(b, nb),
        in_specs=[uspec(0), uspec(1), uspec(2), wspec(0), wspec(1), wspec(2), bspec(0), bspec(1), bspec(2)],
        out_specs=[ospec, ospec],
        out_shape=[jax.ShapeDtypeStruct((b, s, c), BF16), jax.ShapeDtypeStruct((b, s, c), BF16)],
        compiler_params=_params(("parallel", "parallel"), 48),
        name="hyena_gate",
    )(proj3, proj3, proj3, conv_w, conv_w, conv_w, conv_b, conv_b, conv_b)


def _filter_kernel(z_ref, w1_ref, b1_ref, fr_ref, w2_ref, b2_ref, w3_ref, dl_ref, o_ref, *, tr):
    hi = lax.Precision.HIGHEST
    fr = fr_ref[...]
    h = jnp.sin(fr * (jnp.dot(z_ref[...], w1_ref[...], precision=hi, preferred_element_type=F32) + b1_ref[...]))
    h = jnp.sin(fr * (jnp.dot(h, w2_ref[...], precision=hi, preferred_element_type=F32) + b2_ref[...]))
    h = jnp.dot(h, w3_ref[...], precision=hi, preferred_element_type=F32)
    row = pl.program_id(0) * tr + lax.broadcasted_iota(jnp.int32, (tr, 1), 0)
    t = row.astype(F32) * (1.0 / (SEQ - 1))
    decay = jnp.exp(-t * dl_ref[...]) + DECAY_SHIFT
    c = HYENA_WIDTH
    o_ref[:, :c] = (h[:, :c] * decay).astype(BF16)
    o_ref[:, c:] = jnp.where(row == 0, 0.0, h[:, c:] * decay).astype(BF16)


def _filters(zpad, w1pad, b1, freq, w2, b2, w3, deltas, tr=256):
    L = SEQ
    c2 = 2 * HYENA_WIDTH
    full = lambda shape: pl.BlockSpec(shape, lambda i: (0, 0))
    return pl.pallas_call(
        functools.partial(_filter_kernel, tr=tr),
        grid=(L // tr,),
        in_specs=[pl.BlockSpec((tr, LANE), lambda i: (i, 0)),
                  full((LANE, FILTER_HIDDEN)), full((1, FILTER_HIDDEN)), full((1, FILTER_HIDDEN)),
                  full((FILTER_HIDDEN, FILTER_HIDDEN)), full((1, FILTER_HIDDEN)),
                  full((FILTER_HIDDEN, c2)), full((1, HYENA_WIDTH))],
        out_specs=pl.BlockSpec((tr, c2), lambda i: (i, 0)),
        out_shape=jax.ShapeDtypeStruct((L, c2), BF16),
        compiler_params=_params(("parallel",), 48),
        name="hyena_filters",
    )(zpad, w1pad, b1, freq, w2, b2, w3, deltas)


def _kspec_kernel(f_ref, h_ref, o_ref):
    c = HYENA_WIDTH
    t = DFT_TILE
    p = jnp.dot(f_ref[0].astype(BF16), h_ref[...], preferred_element_type=F32)
    row = pl.program_id(0) * t + lax.broadcasted_iota(jnp.int32, (t, 1), 0)
    o_ref[0] = p[:t, :c] + p[:t, c:]
    o_ref[1] = jnp.where(row == 0, p[t:, :c] + p[t:, c:], p[t:, :c] - p[t:, c:])


def _kspec(table, hcat):
    L = SEQ
    c = HYENA_WIDTH
    t = DFT_TILE
    return pl.pallas_call(
        _kspec_kernel,
        grid=(L // t,),
        in_specs=[pl.BlockSpec((1, 2 * t, L), lambda i: (i, 0, 0)),
                  pl.BlockSpec((L, 2 * c), lambda i: (0, 0), pipeline_mode=pl.Buffered(1))],
        out_specs=pl.BlockSpec((2, t, c), lambda i: (0, i, 0)),
        out_shape=jax.ShapeDtypeStruct((2, L, c), F32),
        compiler_params=_params(("arbitrary",), 56),
        name="hyena_kspec",
    )(table, hcat)


def _fwd_kernel(f_ref, vg_ref, k_ref, y_ref, fb_ref):
    t = DFT_TILE

    @pl.when(pl.program_id(1) == 0)
    def _():
        fb_ref[...] = f_ref[0].astype(BF16)

    x = jnp.dot(fb_ref[...], vg_ref[0], preferred_element_type=F32)
    xr = x[:t]
    xi = x[t:]
    kr = k_ref[0]
    ki = k_ref[1]
    row = pl.program_id(0) * t + lax.broadcasted_iota(jnp.int32, (t, 1), 0)
    first = row == 0
    yr = jnp.where(first, xr * kr, xr * kr - xi * ki)
    yi = jnp.where(first, xi * ki, xr * ki + xi * kr)
    y_ref[0, 0] = yr.astype(BF16)
    y_ref[0, 1] = yi.astype(BF16)


def _fwd_dft(table, vg, kspec):
    b, L, c = vg.shape
    t = DFT_TILE
    return pl.pallas_call(
        _fwd_kernel,
        grid=(L // t, b),
        in_specs=[pl.BlockSpec((1, 2 * t, L), lambda j, bi: (j, 0, 0), pipeline_mode=pl.Buffered(1)),
                  pl.BlockSpec((1, L, c), lambda j, bi: (bi, 0, 0)),
                  pl.BlockSpec((2, t, c), lambda j, bi: (0, j, 0))],
        out_specs=pl.BlockSpec((1, 2, t, c), lambda j, bi: (bi, 0, j, 0)),
        out_shape=jax.ShapeDtypeStruct((b, 2, L, c), BF16),
        scratch_shapes=[pltpu.VMEM((2 * t, L), BF16)],
        compiler_params=_params(("arbitrary", "arbitrary"), 56),
        name="hyena_fwd_dft",
    )(table, vg, kspec)


def _inv_kernel(f_ref, y_ref, vg_ref, x1_ref, d_ref, gain_ref, o_ref, g_ref):
    t = DFT_TILE
    L = SEQ

    @pl.when(pl.program_id(1) == 0)
    def _():
        row = pl.program_id(0) * t + lax.broadcasted_iota(jnp.int32, (t, 1), 0)
        col = lax.broadcasted_iota(jnp.int32, (1, L), 1)
        sign = (1 - 2 * (row & 1)).astype(F32)
        gc = f_ref[0, :t, :] * jnp.where(col == 0, 1.0 / FFT_N, 2.0 / FFT_N)
        gs = jnp.where(col == 0, sign * (1.0 / FFT_N),
                       jnp.where(row == 0, 0.0, f_ref[0, t:, :] * (2.0 / FFT_N)))
        g_ref[:, :L] = gc.astype(BF16)
        g_ref[:, L:] = gs.astype(BF16)

    y = jnp.dot(g_ref[...], y_ref[0], preferred_element_type=F32)
    vg = vg_ref[0].astype(F32)
    y = (y + vg * d_ref[...]) * x1_ref[0].astype(F32)
    ms = jnp.mean(y * y, axis=-1, keepdims=True)
    o_ref[0] = (y * lax.rsqrt(ms + EPS) * gain_ref[...]).astype(BF16)


def _inv_dft(table, yspec, vg, x1, d_skip, gain):
    b, L, c = vg.shape
    t = DFT_TILE
    return pl.pallas_call(
        _inv_kernel,
        grid=(L // t, b),
        in_specs=[pl.BlockSpec((1, 2 * t, L), lambda j, bi: (j, 0, 0), pipeline_mode=pl.Buffered(1)),
                  pl.BlockSpec((1, 2 * L, c), lambda j, bi: (bi, 0, 0)),
                  pl.BlockSpec((1, t, c), lambda j, bi: (bi, j, 0)),
                  pl.BlockSpec((1, t, c), lambda j, bi: (bi, j, 0)),
                  pl.BlockSpec((1, c), lambda j, bi: (0, 0)),
                  pl.BlockSpec((1, c), lambda j, bi: (0, 0))],
        out_specs=pl.BlockSpec((1, t, c), lambda j, bi: (bi, j, 0)),
        out_shape=jax.ShapeDtypeStruct((b, L, c), BF16),
        scratch_shapes=[pltpu.VMEM((t, 2 * L), BF16)],
        compiler_params=_params(("arbitrary", "arbitrary"), 56),
        name="hyena_inv_dft",
    )(table, yspec, vg, x1, d_skip, gain)


def _layer_norm(y, g, b):
    mu = jnp.mean(y, axis=-1, keepdims=True)
    yc = y - mu
    var = jnp.mean(yc * yc, axis=-1, keepdims=True)
    return yc * lax.rsqrt(var + EPS) * g + b


def _outproj_kernel(att_ref, hy_ref, w_ref, x_ref, g_ref, b_ref, o_ref):
    a = ATTN_WIDTH
    mix = jnp.dot(att_ref[...], w_ref[:a, :], preferred_element_type=F32)
    mix = mix + jnp.dot(hy_ref[...], w_ref[a:, :], preferred_element_type=F32)
    y = ALPHA * x_ref[...] + mix
    o_ref[...] = _layer_norm(y, g_ref[...], b_ref[...])


def _outproj(att2d, hy2d, w_bf16, x2d, g, b, tm=512):
    m, d = x2d.shape
    return pl.pallas_call(
        _outproj_kernel,
        grid=(m // tm,),
        in_specs=[pl.BlockSpec((tm, ATTN_WIDTH), lambda i: (i, 0)),
                  pl.BlockSpec((tm, HYENA_WIDTH), lambda i: (i, 0)),
                  pl.BlockSpec((d, d), lambda i: (0, 0)),
                  pl.BlockSpec((tm, d), lambda i: (i, 0)),
                  pl.BlockSpec((1, d), lambda i: (0, 0)),
                  pl.BlockSpec((1, d), lambda i: (0, 0))],
        out_specs=pl.BlockSpec((tm, d), lambda i: (i, 0)),
        out_shape=jax.ShapeDtypeStruct((m, d), F32),
        compiler_params=_params(("parallel",), 56),
        name="outproj_ln",
    )(att2d, hy2d, w_bf16, x2d, g, b)


def _ffn_kernel(x_ref, w1_ref, w2_ref, g_ref, b_ref, o_ref, xb_ref, acc_ref):
    f = pl.program_id(1)

    @pl.when(f == 0)
    def _():
        xb_ref[...] = x_ref[...].astype(BF16)
        acc_ref[...] = jnp.zeros_like(acc_ref)

    h = jnp.dot(xb_ref[...], w1_ref[...], preferred_element_type=F32)
    h = jnp.square(jnp.maximum(h, 0.0)).astype(BF16)
    acc_ref[...] += jnp.dot(h, w2_ref[...], preferred_element_type=F32)

    @pl.when(f == pl.num_programs(1) - 1)
    def _():
        y = ALPHA * x_ref[...] + acc_ref[...]
        o_ref[...] = _layer_norm(y, g_ref[...], b_ref[...])


def _ffn(x2d, w1_bf16, w2_bf16, g, b, tm=512, tf=512):
    m, d = x2d.shape
    dff = w1_bf16.shape[1]
    return pl.pallas_call(
        _ffn_kernel,
        grid=(m // tm, dff // tf),
        in_specs=[pl.BlockSpec((tm, d), lambda i, j: (i, 0)),
                  pl.BlockSpec((d, tf), lambda i, j: (0, j)),
                  pl.BlockSpec((tf, d), lambda i, j: (j, 0)),
                  pl.BlockSpec((1, d), lambda i, j: (0, 0)),
                  pl.BlockSpec((1, d), lambda i, j: (0, 0))],
        out_specs=pl.BlockSpec((tm, d), lambda i, j: (i, 0)),
        out_shape=jax.ShapeDtypeStruct((m, d), F32),
        scratch_shapes=[pltpu.VMEM((tm, d), BF16), pltpu.VMEM((tm, d), F32)],
        compiler_params=_params(("parallel", "arbitrary"), 56),
        name="ffn_ln",
    )(x2d, w1_bf16, w2_bf16, g, b)


def kernel(x, w_in, lambda_q1, lambda_k1, lambda_q2, lambda_k2, subln_g, conv_w, conv_b, filt_w1, filt_b1,
           filt_freq, filt_w2, filt_b2, filt_w3, hyena_skip, hyena_gain, w_out, ln1_g, ln1_b, w_ff1, w_ff2,
           ln2_g, ln2_b):
    B, S, D = x.shape
    assert (B, S, D) == (BATCH, SEQ, D_MODEL)
    zpad_np, deltas_np = _filter_tables()
    dft_tab = jnp.asarray(_dft_table())
    zpad = jnp.asarray(zpad_np)
    deltas = jnp.asarray(deltas_np)
    row = lambda v: v.astype(F32).reshape(1, -1)

    x2d = x.reshape(B * S, D)
    for l in range(DEPTH):
        lam_init = 0.8 - 0.6 * math.exp(-0.3 * l)
        lam = (jnp.exp(jnp.sum(lambda_q1[l].astype(F32) * lambda_k1[l].astype(F32)))
               - jnp.exp(jnp.sum(lambda_q2[l].astype(F32) * lambda_k2[l].astype(F32)))
               + lam_init).reshape(1)

        proj = _inproj(x2d, w_in[l].astype(BF16))
        proj3 = proj.reshape(B, S, IN_COLS)

        att = _attention(proj3, lam, row(subln_g[l]), lam_init)

        vg, x1 = _gate(proj3, conv_w[l].astype(F32), row(conv_b[l]))
        w1pad = jnp.zeros((LANE, FILTER_HIDDEN), F32).at[:FILTER_EMB].set(filt_w1[l].astype(F32))
        hcat = _filters(zpad, w1pad, row(filt_b1[l]), row(filt_freq[l]), filt_w2[l].astype(F32),
                        row(filt_b2[l]), filt_w3[l].astype(F32), deltas)
        kspec = _kspec(dft_tab, hcat)
        yspec = _fwd_dft(dft_tab, vg, kspec)
        hy = _inv_dft(dft_tab, yspec.reshape(B, 2 * S, HYENA_WIDTH), vg, x1, row(hyena_skip[l]),
                      row(hyena_gain[l]))

        x2d = _outproj(att.reshape(B * S, ATTN_WIDTH), hy.reshape(B * S, HYENA_WIDTH), w_out[l].astype(BF16),
                       x2d, row(ln1_g[l]), row(ln1_b[l]))
        x2d = _ffn(x2d, w_ff1[l].astype(BF16), w_ff2[l].astype(BF16), row(ln2_g[l]), row(ln2_b[l]))
    return x2d.reshape(B, S, D)
```

```python
import functools
import math

import numpy as np
import jax
import jax.numpy as jnp
from jax import lax
from jax.experimental import pallas as pl
from jax.experimental.pallas import tpu as pltpu

D_MODEL = 2048
BATCH = 8
SEQ = 2048
DEPTH = 1
ATTN_WIDTH = D_MODEL // 2
HYENA_WIDTH = D_MODEL - ATTN_WIDTH
N_HEADS = 8
HEAD_DIM = ATTN_WIDTH // N_HEADS // 2
V_HEAD_DIM = 2 * HEAD_DIM
SHORT_CONV = 3
FILTER_EMB = 33
FILTER_HIDDEN = 64
DECAY_FAST = 0.3
DECAY_SLOW = 1.5
DECAY_TARGET = 1e-2
DECAY_SHIFT = 0.0
D_FF = 4 * D_MODEL
ALPHA = (2.0 * DEPTH) ** 0.25
EPS = 1e-5
IN_COLS = 3 * ATTN_WIDTH + 3 * HYENA_WIDTH
FFT_N = 2 * SEQ
LANE = 128
MXU_DIM = 256
DFT_TILE = 512
FFN_TILE = 512
ATTN_TILE = 512
ATTN_KEYS = 512
LOG2E = math.log2(math.e)
Q_PRESCALE = HEAD_DIM ** -0.5 * LOG2E

BF16 = jnp.bfloat16
F32 = jnp.float32

_MIB = 1024 * 1024


def _params(semantics, vmem_mib):
    return pltpu.CompilerParams(dimension_semantics=semantics, vmem_limit_bytes=vmem_mib * _MIB)


@functools.lru_cache(maxsize=None)
def _dft_table():
    k = np.arange(SEQ, dtype=np.int64)
    phase = (k[:, None] * k[None, :]) % FFT_N
    ang = (2.0 * np.pi / FFT_N) * phase
    c = np.cos(ang)
    s = -np.sin(ang)
    s[0, :] = np.where(k % 2 == 0, 1.0, -1.0)
    nt = SEQ // DFT_TILE
    tiled = np.concatenate([c.reshape(nt, DFT_TILE, SEQ), s.reshape(nt, DFT_TILE, SEQ)], axis=1)
    return tiled.astype(np.float32)


@functools.lru_cache(maxsize=None)
def _filter_tables():
    L = SEQ
    t = np.linspace(0.0, 1.0, L)[:, None]
    bands = (FILTER_EMB - 1) // 2
    w = 2.0 * np.pi * np.arange(L, dtype=np.float64)[:, None] / L
    f = np.linspace(1e-4, bands - 1, bands)[None, :]
    z = np.concatenate([t, np.cos(f * w), -np.sin(f * w)], axis=-1)
    zpad = np.zeros((L, LANE), np.float32)
    zpad[:, :FILTER_EMB] = z
    max_decay = math.log(DECAY_TARGET) / DECAY_FAST
    min_decay = math.log(DECAY_TARGET) / DECAY_SLOW
    deltas = np.abs(np.linspace(min_decay, max_decay, HYENA_WIDTH))[None, :].astype(np.float32)
    return zpad, deltas


def _inproj_kernel(x_ref, w_ref, o_ref, xb_ref):
    @pl.when(pl.program_id(1) == 0)
    def _():
        xb_ref[...] = x_ref[...].astype(BF16)

    scale = jnp.where(pl.program_id(1) == 0, Q_PRESCALE, 1.0)
    acc = jnp.dot(xb_ref[...], w_ref[...], preferred_element_type=F32)
    o_ref[...] = (acc * scale).astype(BF16)


def _inproj(x2d, w_bf16, tm=1024, tn=ATTN_WIDTH):
    m, k = x2d.shape
    n = w_bf16.shape[1]
    return pl.pallas_call(
        _inproj_kernel,
        grid=(m // tm, n // tn),
        in_specs=[pl.BlockSpec((tm, k), lambda i, j: (i, 0)),
                  pl.BlockSpec((k, tn), lambda i, j: (0, j))],
        out_specs=pl.BlockSpec((tm, tn), lambda i, j: (i, j)),
        out_shape=jax.ShapeDtypeStruct((m, n), BF16),
        scratch_shapes=[pltpu.VMEM((tm, k), BF16)],
        compiler_params=_params(("parallel", "arbitrary"), 48),
        name="inproj",
    )(x2d, w_bf16)


@functools.lru_cache(maxsize=None)
def _alibi_tables():
    pos = np.arange(SEQ)
    hi = ((pos >> 8) << 8).astype(np.float64)
    lo = (pos & 255).astype(np.float64)
    augq = np.zeros((N_HEADS, SEQ, V_HEAD_DIM), np.float64)
    augk = np.zeros((N_HEADS, SEQ, V_HEAD_DIM), np.float64)
    t = ATTN_TILE
    ahead = np.maximum(np.arange(t)[None, :] - np.arange(t)[:, None], 0).astype(np.float64)
    dg = np.zeros((N_HEADS, t, t), np.float64)
    for h in range(N_HEADS):
        c = 2.0 ** (-8.0 * (h + 1) / N_HEADS) * LOG2E
        rest = c
        for p in range(3):
            piece = float(np.float32(rest).astype(BF16).astype(np.float64))
            rest -= piece
            for base in (0, HEAD_DIM):
                augq[h, :, base + 2 * p] = piece
                augq[h, :, base + 2 * p + 1] = piece
                augk[h, :, base + 2 * p] = hi
                augk[h, :, base + 2 * p + 1] = lo
                augq[h, :, base + 6 + 2 * p] = -hi
                augq[h, :, base + 6 + 2 * p + 1] = -lo
                augk[h, :, base + 6 + 2 * p] = piece
                augk[h, :, base + 6 + 2 * p + 1] = piece
        dg[h] = -2.0 * c * ahead
    return augq.astype(BF16), augk.astype(BF16), dg.astype(np.float32)


def _attn_kernel(lam_ref, q_ref, k_ref, v_ref, aq_ref, ak_ref, dg_ref, g_ref, o_ref,
                 kt_ref, sa_ref, sb_ref, vx_ref, *, lam_init, n_items):
    t = ATTN_TILE
    tk = ATTN_KEYS
    nkb = SEQ // tk
    per_q = t // tk
    j = pl.program_id(1)
    qb = lax.rem(jnp.minimum(j, n_items - 1), SEQ // t)
    lam = lam_ref[0]
    lane = lax.broadcasted_iota(jnp.int32, (1, V_HEAD_DIM), 1)
    first_half = lane < HEAD_DIM
    nt_dims = (((1,), (1,)), ((), ()))

    @pl.when(j == 0)
    def _():
        sb_ref[...] = jnp.zeros_like(sb_ref)

    @pl.when(qb == 0)
    def _():
        k = k_ref[0]
        ak = ak_ref[0]
        kt_ref[0] = jnp.where(first_half, k, ak)
        kt_ref[1] = jnp.where(first_half, ak, k)

    @pl.when(lax.rem(jnp.maximum(j - 1, 0), SEQ // t) == 0)
    def _():
        vx_ref[:, :V_HEAD_DIM] = v_ref[0]
        ones_col = jnp.where(lane == 0, 1.0, 0.0).astype(BF16)
        vx_ref[:, V_HEAD_DIM:] = jnp.broadcast_to(ones_col, (SEQ, V_HEAD_DIM))

    def step(s_new, s_old):
        q = q_ref[0]
        aq = aq_ref[0]
        naq = -aq
        halves = (first_half, jnp.logical_not(first_half))
        q_left = [jnp.where(keep, q, aq) for keep in halves]
        q_right = [jnp.where(keep, q, naq) for keep in halves]

        def score_block(c, r):
            kb = qb * per_q + r
            kb = jnp.where(kb >= nkb, kb - nkb, kb)
            if r < per_q:
                lhs = q_left[c]
            else:
                lhs = jnp.where(qb * per_q + r >= nkb, q_left[c], q_right[c])
            s = lax.dot_general(lhs, kt_ref[c, pl.ds(pl.multiple_of(kb * tk, tk), tk), :], nt_dims,
                                preferred_element_type=F32)
            if r < per_q:
                s = s + dg_ref[0, :, r * tk:(r + 1) * tk]
            s_new[c, kb] = s

        row_max = [None, None]
        for r in range(nkb):
            score_block(0, r)
            for c in range(2):
                bm = jnp.max(s_old[c, r], axis=-1, keepdims=True)
                row_max[c] = bm if r == 0 else jnp.maximum(row_max[c], bm)

        outs = []
        for c in range(2):
            chunks = []
            for kb in range(nkb):
                for lo in range(0, tk, MXU_DIM):
                    chunks.append(jnp.exp2(s_old[c, kb, :, lo:lo + MXU_DIM] - row_max[c]).astype(BF16))
            acc = jnp.dot(jnp.concatenate(chunks, axis=1), vx_ref[...], preferred_element_type=F32)
            outs.append(acc[:, :V_HEAD_DIM] / acc[:, V_HEAD_DIM:V_HEAD_DIM + 1])

        for r in range(nkb):
            score_block(1, r)

        a = outs[0] - lam * outs[1]
        ms = jnp.mean(a * a, axis=-1, keepdims=True)
        y = a * lax.rsqrt(ms + EPS) * g_ref[...] * (1.0 - lam_init)
        o_ref[0] = y.astype(BF16)

    @pl.when(lax.rem(j, 2) == 0)
    def _():
        step(sa_ref, sb_ref)

    @pl.when(lax.rem(j, 2) == 1)
    def _():
        step(sb_ref, sa_ref)


def _attention(proj3, lam, subln_g, lam_init):
    b, s, _ = proj3.shape
    nh = N_HEADS
    t = ATTN_TILE
    nq = s // t
    n_items = b * nq
    augq_np, augk_np, dg_np = _alibi_tables()
    kern = functools.partial(_attn_kernel, lam_init=lam_init, n_items=n_items)
    smem = pl.BlockSpec(memory_space=pltpu.SMEM)
    cur = lambda j: jnp.minimum(j, n_items - 1)
    prev = lambda j: jnp.maximum(j - 1, 0)
    score_scratch = pltpu.VMEM((2, s // ATTN_KEYS, t, ATTN_KEYS), F32)
    return pl.pallas_call(
        kern,
        grid=(nh, n_items + 1),
        in_specs=[smem,
                  pl.BlockSpec((1, t, V_HEAD_DIM), lambda hi, j: (cur(j) // nq, cur(j) % nq, hi)),
                  pl.BlockSpec((1, s, V_HEAD_DIM), lambda hi, j: (cur(j) // nq, 0, nh + hi)),
                  pl.BlockSpec((1, s, V_HEAD_DIM), lambda hi, j: (prev(j) // nq, 0, 2 * nh + hi)),
                  pl.BlockSpec((1, t, V_HEAD_DIM), lambda hi, j: (hi, cur(j) % nq, 0)),
                  pl.BlockSpec((1, s, V_HEAD_DIM), lambda hi, j: (hi, 0, 0)),
                  pl.BlockSpec((1, t, t), lambda hi, j: (hi, 0, 0)),
                  pl.BlockSpec((1, V_HEAD_DIM), lambda hi, j: (0, 0))],
        out_specs=pl.BlockSpec((1, t, V_HEAD_DIM), lambda hi, j: (prev(j) // nq, prev(j) % nq, hi)),
        out_shape=jax.ShapeDtypeStruct((b, s, ATTN_WIDTH), BF16),
        scratch_shapes=[pltpu.VMEM((2, s, V_HEAD_DIM), BF16), score_scratch, score_scratch,
                        pltpu.VMEM((s, 2 * V_HEAD_DIM), BF16)],
        compiler_params=_params(("arbitrary", "arbitrary"), 48),
        name="diff_attn",
    )(lam, proj3, proj3, proj3, jnp.asarray(augq_np), jnp.asarray(augk_np), jnp.asarray(dg_np), subln_g)


def _gate_kernel(u1_ref, u2_ref, u3_ref, w1_ref, w2_ref, w3_ref, b1_ref, b2_ref, b3_ref, vg_ref, x1_ref):
    s_len = u1_ref.shape[1]
    row = lax.broadcasted_iota(jnp.int32, (s_len, 1), 0)

    def conv(u_ref, w_ref, b_ref):
        u = u_ref[0].astype(F32)
        prev = jnp.where(row == 0, 0.0, pltpu.roll(u, 1, 0))
        nxt = jnp.where(row == s_len - 1, 0.0, pltpu.roll(u, s_len - 1, 0))
        w = w_ref[...]
        return b_ref[...] + prev * w[0:1] + u * w[1:2] + nxt * w[2:3]

    x1 = conv(u1_ref, w1_ref, b1_ref)
    x2 = conv(u2_ref, w2_ref, b2_ref)
    v = conv(u3_ref, w3_ref, b3_ref)
    vg_ref[0] = (v * x2).astype(BF16)
    x1_ref[0] = x1.astype(BF16)


def _gate(proj3, conv_w, conv_b, tc=256):
    b, s, _ = proj3.shape
    c = HYENA_WIDTH
    nb = c // tc
    base = 3 * ATTN_WIDTH // tc

    def uspec(part):
        return pl.BlockSpec((1, s, tc), lambda bi, ci: (bi, 0, base + part * nb + ci))

    def wspec(part):
        return pl.BlockSpec((SHORT_CONV, tc), lambda bi, ci: (0, part * nb + ci))

    def bspec(part):
        return pl.BlockSpec((1, tc), lambda bi, ci: (0, part * nb + ci))

    ospec = pl.BlockSpec((1, s, tc), lambda bi, ci: (bi, 0, ci))
    return pl.pallas_call(
        _gate_kernel,
        grid=(b, nb),
        in_specs=[uspec(0), uspec(1), uspec(2), wspec(0), wspec(1), wspec(2), bspec(0), bspec(1), bspec(2)],
        out_specs=[ospec, ospec],
        out_shape=[jax.ShapeDtypeStruct((b, s, c), BF16), jax.ShapeDtypeStruct((b, s, c), BF16)],
        compiler_params=_params(("parallel", "parallel"), 48),
        name="hyena_gate",
    )(proj3, proj3, proj3, conv_w, conv_w, conv_w, conv_b, conv_b, conv_b)


def _filter_kernel(z_ref, w1_ref, b1_ref, fr_ref, w2_ref, b2_ref, w3_ref, dl_ref, o_ref, *, tr):
    hi = lax.Precision.HIGHEST
    fr = fr_ref[...]
    h = jnp.sin(fr * (jnp.dot(z_ref[...], w1_ref[...], precision=hi, preferred_element_type=F32) + b1_ref[...]))
    h = jnp.sin(fr * (jnp.dot(h, w2_ref[...], precision=hi, preferred_element_type=F32) + b2_ref[...]))
    h = jnp.dot(h, w3_ref[...], precision=hi, preferred_element_type=F32)
    row = pl.program_id(0) * tr + lax.broadcasted_iota(jnp.int32, (tr, 1), 0)
    t = row.astype(F32) * (1.0 / (SEQ - 1))
    decay = jnp.exp(-t * dl_ref[...]) + DECAY_SHIFT
    c = HYENA_WIDTH
    o_ref[:, :c] = (h[:, :c] * decay).astype(BF16)
    o_ref[:, c:] = jnp.where(row == 0, 0.0, h[:, c:] * decay).astype(BF16)


def _filters(zpad, w1pad, b1, freq, w2, b2, w3, deltas, tr=256):
    L = SEQ
    c2 = 2 * HYENA_WIDTH
    full = lambda shape: pl.BlockSpec(shape, lambda i: (0, 0))
    return pl.pallas_call(
        functools.partial(_filter_kernel, tr=tr),
        grid=(L // tr,),
        in_specs=[pl.BlockSpec((tr, LANE), lambda i: (i, 0)),
                  full((LANE, FILTER_HIDDEN)), full((1, FILTER_HIDDEN)), full((1, FILTER_HIDDEN)),
                  full((FILTER_HIDDEN, FILTER_HIDDEN)), full((1, FILTER_HIDDEN)),
                  full((FILTER_HIDDEN, c2)), full((1, HYENA_WIDTH))],
        out_specs=pl.BlockSpec((tr, c2), lambda i: (i, 0)),
        out_shape=jax.ShapeDtypeStruct((L, c2), BF16),
        compiler_params=_params(("parallel",), 48),
        name="hyena_filters",
    )(zpad, w1pad, b1, freq, w2, b2, w3, deltas)


def _kspec_kernel(f_ref, h_ref, o_ref):
    c = HYENA_WIDTH
    t = DFT_TILE
    p = jnp.dot(f_ref[0].astype(BF16), h_ref[...], preferred_element_type=F32)
    row = pl.program_id(0) * t + lax.broadcasted_iota(jnp.int32, (t, 1), 0)
    o_ref[0] = p[:t, :c] + p[:t, c:]
    o_ref[1] = jnp.where(row == 0, p[t:, :c] + p[t:, c:], p[t:, :c] - p[t:, c:])


def _kspec(table, hcat):
    L = SEQ
    c = HYENA_WIDTH
    t = DFT_TILE
    return pl.pallas_call(
        _kspec_kernel,
        grid=(L // t,),
        in_specs=[pl.BlockSpec((1, 2 * t, L), lambda i: (i, 0, 0)),
                  pl.BlockSpec((L, 2 * c), lambda i: (0, 0), pipeline_mode=pl.Buffered(1))],
        out_specs=pl.BlockSpec((2, t, c), lambda i: (0, i, 0)),
        out_shape=jax.ShapeDtypeStruct((2, L, c), F32),
        compiler_params=_params(("arbitrary",), 56),
        name="hyena_kspec",
    )(table, hcat)


def _fwd_kernel(f_ref, vg_ref, k_ref, y_ref, fb_ref):
    t = DFT_TILE

    @pl.when(pl.program_id(1) == 0)
    def _():
        fb_ref[...] = f_ref[0].astype(BF16)

    x = jnp.dot(fb_ref[...], vg_ref[0], preferred_element_type=F32)
    xr = x[:t]
    xi = x[t:]
    kr = k_ref[0]
    ki = k_ref[1]
    row = pl.program_id(0) * t + lax.broadcasted_iota(jnp.int32, (t, 1), 0)
    first = row == 0
    yr = jnp.where(first, xr * kr, xr * kr - xi * ki)
    yi = jnp.where(first, xi * ki, xr * ki + xi * kr)
    y_ref[0, 0] = yr.astype(BF16)
    y_ref[0, 1] = yi.astype(BF16)


def _fwd_dft(table, vg, kspec):
    b, L, c = vg.shape
    t = DFT_TILE
    return pl.pallas_call(
        _fwd_kernel,
        grid=(L // t, b),
        in_specs=[pl.BlockSpec((1, 2 * t, L), lambda j, bi: (j, 0, 0), pipeline_mode=pl.Buffered(1)),
                  pl.BlockSpec((1, L, c), lambda j, bi: (bi, 0, 0)),
                  pl.BlockSpec((2, t, c), lambda j, bi: (0, j, 0))],
        out_specs=pl.BlockSpec((1, 2, t, c), lambda j, bi: (bi, 0, j, 0)),
        out_shape=jax.ShapeDtypeStruct((b, 2, L, c), BF16),
        scratch_shapes=[pltpu.VMEM((2 * t, L), BF16)],
        compiler_params=_params(("arbitrary", "arbitrary"), 56),
        name="hyena_fwd_dft",
    )(table, vg, kspec)


def _inv_kernel(f_ref, y_ref, vg_ref, x1_ref, d_ref, gain_ref, o_ref, g_ref):
    t = DFT_TILE
    L = SEQ

    @pl.when(pl.program_id(1) == 0)
    def _():
        row = pl.program_id(0) * t + lax.broadcasted_iota(jnp.int32, (t, 1), 0)
        col = lax.broadcasted_iota(jnp.int32, (1, L), 1)
        sign = (1 - 2 * (row & 1)).astype(F32)
        gc = f_ref[0, :t, :] * jnp.where(col == 0, 1.0 / FFT_N, 2.0 / FFT_N)
        gs = jnp.where(col == 0, sign * (1.0 / FFT_N),
                       jnp.where(row == 0, 0.0, f_ref[0, t:, :] * (2.0 / FFT_N)))
        g_ref[:, :L] = gc.astype(BF16)
        g_ref[:, L:] = gs.astype(BF16)

    y = jnp.dot(g_ref[...], y_ref[0], preferred_element_type=F32)
    vg = vg_ref[0].astype(F32)
    y = (y + vg * d_ref[...]) * x1_ref[0].astype(F32)
    ms = jnp.mean(y * y, axis=-1, keepdims=True)
    o_ref[0] = (y * lax.rsqrt(ms + EPS) * gain_ref[...]).astype(BF16)


def _inv_dft(table, yspec, vg, x1, d_skip, gain):
    b, L, c = vg.shape
    t = DFT_TILE
    return pl.pallas_call(
        _inv_kernel,
        grid=(L // t, b),
        in_specs=[pl.BlockSpec((1, 2 * t, L), lambda j, bi: (j, 0, 0), pipeline_mode=pl.Buffered(1)),
                  pl.BlockSpec((1, 2 * L, c), lambda j, bi: (bi, 0, 0)),
                  pl.BlockSpec((1, t, c), lambda j, bi: (bi, j, 0)),
                  pl.BlockSpec((1, t, c), lambda j, bi: (bi, j, 0)),
                  pl.BlockSpec((1, c), lambda j, bi: (0, 0)),
                  pl.BlockSpec((1, c), lambda j, bi: (0, 0))],
        out_specs=pl.BlockSpec((1, t, c), lambda j, bi: (bi, j, 0)),
        out_shape=jax.ShapeDtypeStruct((b, L, c), BF16),
        scratch_shapes=[pltpu.VMEM((t, 2 * L), BF16)],
        compiler_params=_params(("arbitrary", "arbitrary"), 56),
        name="hyena_inv_dft",
    )(table, yspec, vg, x1, d_skip, gain)


def _layer_norm(y, g, b):
    mu = jnp.mean(y, axis=-1, keepdims=True)
    yc = y - mu
    var = jnp.mean(yc * yc, axis=-1, keepdims=True)
    return yc * lax.rsqrt(var + EPS) * g + b


def _outproj_kernel(att_ref, hy_ref, w_ref, x_ref, g_ref, b_ref, o_ref):
    a = ATTN_WIDTH
    mix = jnp.dot(att_ref[...], w_ref[:a, :], preferred_element_type=F32)
    mix = mix + jnp.dot(hy_ref[...], w_ref[a:, :], preferred_element_type=F32)
    y = ALPHA * x_ref[...] + mix
    o_ref[...] = _layer_norm(y, g_ref[...], b_ref[...])


def _outproj(att2d, hy2d, w_bf16, x2d, g, b, tm=512):
    m, d = x2d.shape
    return pl.pallas_call(
        _outproj_kernel,
        grid=(m // tm,),
        in_specs=[pl.BlockSpec((tm, ATTN_WIDTH), lambda i: (i, 0)),
                  pl.BlockSpec((tm, HYENA_WIDTH), lambda i: (i, 0)),
                  pl.BlockSpec((d, d), lambda i: (0, 0)),
                  pl.BlockSpec((tm, d), lambda i: (i, 0)),
                  pl.BlockSpec((1, d), lambda i: (0, 0)),
                  pl.BlockSpec((1, d), lambda i: (0, 0))],
        out_specs=pl.BlockSpec((tm, d), lambda i: (i, 0)),
        out_shape=jax.ShapeDtypeStruct((m, d), F32),
        compiler_params=_params(("parallel",), 56),
        name="outproj_ln",
    )(att2d, hy2d, w_bf16, x2d, g, b)


def _ffn_kernel(x_ref, w1_ref, w2_ref, g_ref, b_ref, o_ref, xb_ref):
    f = pl.program_id(1)

    @pl.when(f == 0)
    def _():
        x = x_ref[...]
        xb_ref[...] = x.astype(BF16)
        o_ref[...] = ALPHA * x

    h = jnp.dot(xb_ref[...], w1_ref[0], preferred_element_type=F32)
    h = jnp.square(jnp.maximum(h, 0.0)).astype(BF16)
    o_ref[...] += jnp.dot(h, w2_ref[...], preferred_element_type=F32)

    @pl.when(f == pl.num_programs(1) - 1)
    def _():
        o_ref[...] = _layer_norm(o_ref[...], g_ref[...], b_ref[...])


def _ffn(x2d, w1_tiles, w2_bf16, g, b, tm=1024):
    m, d = x2d.shape
    nf, _, tf = w1_tiles.shape
    return pl.pallas_call(
        _ffn_kernel,
        grid=(m // tm, nf),
        in_specs=[pl.BlockSpec((tm, d), lambda i, j: (i, 0)),
                  pl.BlockSpec((1, d, tf), lambda i, j: (j, 0, 0)),
                  pl.BlockSpec((tf, d), lambda i, j: (j, 0)),
                  pl.BlockSpec((1, d), lambda i, j: (0, 0)),
                  pl.BlockSpec((1, d), lambda i, j: (0, 0))],
        out_specs=pl.BlockSpec((tm, d), lambda i, j: (i, 0)),
        out_shape=jax.ShapeDtypeStruct((m, d), F32),
        scratch_shapes=[pltpu.VMEM((tm, d), BF16)],
        compiler_params=_params(("parallel", "arbitrary"), 56),
        name="ffn_ln",
    )(x2d, w1_tiles, w2_bf16, g, b)


def kernel(x, w_in, lambda_q1, lambda_k1, lambda_q2, lambda_k2, subln_g, conv_w, conv_b, filt_w1, filt_b1,
           filt_freq, filt_w2, filt_b2, filt_w3, hyena_skip, hyena_gain, w_out, ln1_g, ln1_b, w_ff1, w_ff2,
           ln2_g, ln2_b):
    B, S, D = x.shape
    assert (B, S, D) == (BATCH, SEQ, D_MODEL)
    zpad_np, deltas_np = _filter_tables()
    dft_tab = jnp.asarray(_dft_table())
    zpad = jnp.asarray(zpad_np)
    deltas = jnp.asarray(deltas_np)
    row = lambda v: v.astype(F32).reshape(1, -1)

    x2d = x.reshape(B * S, D)
    for l in range(DEPTH):
        lam_init = 0.8 - 0.6 * math.exp(-0.3 * l)
        lam = (jnp.exp(jnp.sum(lambda_q1[l].astype(F32) * lambda_k1[l].astype(F32)))
               - jnp.exp(jnp.sum(lambda_q2[l].astype(F32) * lambda_k2[l].astype(F32)))
               + lam_init).reshape(1)

        proj = _inproj(x2d, w_in[l].astype(BF16))
        proj3 = proj.reshape(B, S, IN_COLS)

        att = _attention(proj3, lam, row(subln_g[l]), lam_init)

        vg, x1 = _gate(proj3, conv_w[l].astype(F32), row(conv_b[l]))
        w1pad = jnp.zeros((LANE, FILTER_HIDDEN), F32).at[:FILTER_EMB].set(filt_w1[l].astype(F32))
        hcat = _filters(zpad, w1pad, row(filt_b1[l]), row(filt_freq[l]), filt_w2[l].astype(F32),
                        row(filt_b2[l]), filt_w3[l].astype(F32), deltas)
        kspec = _kspec(dft_tab, hcat)
        yspec = _fwd_dft(dft_tab, vg, kspec)
        hy = _inv_dft(dft_tab, yspec.reshape(B, 2 * S, HYENA_WIDTH), vg, x1, row(hyena_skip[l]),
                      row(hyena_gain[l]))

        x2d = _outproj(att.reshape(B * S, ATTN_WIDTH), hy.reshape(B * S, HYENA_WIDTH), w_out[l].astype(BF16),
                       x2d, row(ln1_g[l]), row(ln1_b[l]))
        w1_tiles = w_ff1[l].astype(BF16).reshape(D, D_FF // FFN_TILE, FFN_TILE).transpose(1, 0, 2)
        x2d = _ffn(x2d, w1_tiles, w_ff2[l].astype(BF16), row(ln2_g[l]), row(ln2_b[l]))
    return x2d.reshape(B, S, D)
```

```python
import functools
import math

import numpy as np
import jax
import jax.numpy as jnp
from jax import lax
from jax.experimental import pallas as pl
from jax.experimental.pallas import tpu as pltpu

D_MODEL = 2048
BATCH = 8
SEQ = 2048
DEPTH = 1
ATTN_WIDTH = D_MODEL // 2
HYENA_WIDTH = D_MODEL - ATTN_WIDTH
N_HEADS = 8
HEAD_DIM = ATTN_WIDTH // N_HEADS // 2
V_HEAD_DIM = 2 * HEAD_DIM
SHORT_CONV = 3
FILTER_EMB = 33
FILTER_HIDDEN = 64
DECAY_FAST = 0.3
DECAY_SLOW = 1.5
DECAY_TARGET = 1e-2
DECAY_SHIFT = 0.0
D_FF = 4 * D_MODEL
ALPHA = (2.0 * DEPTH) ** 0.25
EPS = 1e-5
IN_COLS = 3 * ATTN_WIDTH + 3 * HYENA_WIDTH
FFT_N = 2 * SEQ
LANE = 128
MXU_DIM = 256
DFT_TILE = 512
FFN_TILE = 512
ROW_CHUNK = 256
ATTN_TILE = 512
ATTN_KEYS = 512
LOG2E = math.log2(math.e)
Q_PRESCALE = HEAD_DIM ** -0.5 * LOG2E

BF16 = jnp.bfloat16
F32 = jnp.float32

_MIB = 1024 * 1024


def _params(semantics, vmem_mib):
    return pltpu.CompilerParams(dimension_semantics=semantics, vmem_limit_bytes=vmem_mib * _MIB)


@functools.lru_cache(maxsize=None)
def _dft_table():
    k = np.arange(SEQ, dtype=np.int64)
    phase = (k[:, None] * k[None, :]) % FFT_N
    ang = (2.0 * np.pi / FFT_N) * phase
    c = np.cos(ang)
    s = -np.sin(ang)
    s[0, :] = np.where(k % 2 == 0, 1.0, -1.0)
    nt = SEQ // DFT_TILE
    tiled = np.concatenate([c.reshape(nt, DFT_TILE, SEQ), s.reshape(nt, DFT_TILE, SEQ)], axis=1)
    return tiled.astype(np.float32)


@functools.lru_cache(maxsize=None)
def _filter_tables():
    L = SEQ
    t = np.linspace(0.0, 1.0, L)[:, None]
    bands = (FILTER_EMB - 1) // 2
    w = 2.0 * np.pi * np.arange(L, dtype=np.float64)[:, None] / L
    f = np.linspace(1e-4, bands - 1, bands)[None, :]
    z = np.concatenate([t, np.cos(f * w), -np.sin(f * w)], axis=-1)
    zpad = np.zeros((L, LANE), np.float32)
    zpad[:, :FILTER_EMB] = z
    max_decay = math.log(DECAY_TARGET) / DECAY_FAST
    min_decay = math.log(DECAY_TARGET) / DECAY_SLOW
    deltas = np.abs(np.linspace(min_decay, max_decay, HYENA_WIDTH))[None, :].astype(np.float32)
    return zpad, deltas


def _inproj_kernel(x_ref, w_ref, o_ref, xb_ref):
    @pl.when(pl.program_id(1) == 0)
    def _():
        xb_ref[...] = x_ref[...].astype(BF16)

    scale = jnp.where(pl.program_id(1) == 0, Q_PRESCALE, 1.0)
    acc = jnp.dot(xb_ref[...], w_ref[...], preferred_element_type=F32)
    o_ref[...] = (acc * scale).astype(BF16)


def _inproj(x2d, w_bf16, tm=1024, tn=ATTN_WIDTH):
    m, k = x2d.shape
    n = w_bf16.shape[1]
    return pl.pallas_call(
        _inproj_kernel,
        grid=(m // tm, n // tn),
        in_specs=[pl.BlockSpec((tm, k), lambda i, j: (i, 0)),
                  pl.BlockSpec((k, tn), lambda i, j: (0, j))],
        out_specs=pl.BlockSpec((tm, tn), lambda i, j: (i, j)),
        out_shape=jax.ShapeDtypeStruct((m, n), BF16),
        scratch_shapes=[pltpu.VMEM((tm, k), BF16)],
        compiler_params=_params(("parallel", "arbitrary"), 48),
        name="inproj",
    )(x2d, w_bf16)


@functools.lru_cache(maxsize=None)
def _alibi_tables():
    pos = np.arange(SEQ)
    hi = ((pos >> 8) << 8).astype(np.float64)
    lo = (pos & 255).astype(np.float64)
    augq = np.zeros((N_HEADS, SEQ, V_HEAD_DIM), np.float64)
    augk = np.zeros((N_HEADS, SEQ, V_HEAD_DIM), np.float64)
    t = ATTN_TILE
    ahead = np.maximum(np.arange(t)[None, :] - np.arange(t)[:, None], 0).astype(np.float64)
    dg = np.zeros((N_HEADS, t, t), np.float64)
    for h in range(N_HEADS):
        c = 2.0 ** (-8.0 * (h + 1) / N_HEADS) * LOG2E
        rest = c
        for p in range(3):
            piece = float(np.float32(rest).astype(BF16).astype(np.float64))
            rest -= piece
            for base in (0, HEAD_DIM):
                augq[h, :, base + 2 * p] = piece
                augq[h, :, base + 2 * p + 1] = piece
                augk[h, :, base + 2 * p] = hi
                augk[h, :, base + 2 * p + 1] = lo
                augq[h, :, base + 6 + 2 * p] = -hi
                augq[h, :, base + 6 + 2 * p + 1] = -lo
                augk[h, :, base + 6 + 2 * p] = piece
                augk[h, :, base + 6 + 2 * p + 1] = piece
        dg[h] = -2.0 * c * ahead
    return augq.astype(BF16), augk.astype(BF16), dg.astype(np.float32)


def _attn_kernel(lam_ref, q_ref, k_ref, v_ref, aq_ref, ak_ref, dg_ref, g_ref, o_ref,
                 kt_ref, sa_ref, sb_ref, vx_ref, *, lam_init, n_items):
    t = ATTN_TILE
    tk = ATTN_KEYS
    nkb = SEQ // tk
    per_q = t // tk
    j = pl.program_id(1)
    qb = lax.rem(jnp.minimum(j, n_items - 1), SEQ // t)
    lam = lam_ref[0]
    lane = lax.broadcasted_iota(jnp.int32, (1, V_HEAD_DIM), 1)
    first_half = lane < HEAD_DIM
    nt_dims = (((1,), (1,)), ((), ()))

    @pl.when(j == 0)
    def _():
        sb_ref[...] = jnp.zeros_like(sb_ref)

    @pl.when(qb == 0)
    def _():
        k = k_ref[0]
        ak = ak_ref[0]
        kt_ref[0] = jnp.where(first_half, k, ak)
        kt_ref[1] = jnp.where(first_half, ak, k)

    @pl.when(lax.rem(jnp.maximum(j - 1, 0), SEQ // t) == 0)
    def _():
        vx_ref[:, :V_HEAD_DIM] = v_ref[0]
        ones_col = jnp.where(lane == 0, 1.0, 0.0).astype(BF16)
        vx_ref[:, V_HEAD_DIM:] = jnp.broadcast_to(ones_col, (SEQ, V_HEAD_DIM))

    def step(s_new, s_old):
        q = q_ref[0]
        aq = aq_ref[0]
        naq = -aq
        halves = (first_half, jnp.logical_not(first_half))
        q_left = [jnp.where(keep, q, aq) for keep in halves]
        q_right = [jnp.where(keep, q, naq) for keep in halves]

        def score_block(c, r):
            kb = qb * per_q + r
            kb = jnp.where(kb >= nkb, kb - nkb, kb)
            if r < per_q:
                lhs = q_left[c]
            else:
                lhs = jnp.where(qb * per_q + r >= nkb, q_left[c], q_right[c])
            s = lax.dot_general(lhs, kt_ref[c, pl.ds(pl.multiple_of(kb * tk, tk), tk), :], nt_dims,
                                preferred_element_type=F32)
            if r < per_q:
                s = s + dg_ref[0, :, r * tk:(r + 1) * tk]
            s_new[c, kb] = s

        row_max = [None, None]
        for r in range(nkb):
            score_block(0, r)
            for c in range(2):
                bm = jnp.max(s_old[c, r], axis=-1, keepdims=True)
                row_max[c] = bm if r == 0 else jnp.maximum(row_max[c], bm)

        outs = []
        for c in range(2):
            chunks = []
            for kb in range(nkb):
                for lo in range(0, tk, MXU_DIM):
                    chunks.append(jnp.exp2(s_old[c, kb, :, lo:lo + MXU_DIM] - row_max[c]).astype(BF16))
            acc = jnp.dot(jnp.concatenate(chunks, axis=1), vx_ref[...], preferred_element_type=F32)
            outs.append(acc[:, :V_HEAD_DIM] / acc[:, V_HEAD_DIM:V_HEAD_DIM + 1])

        for r in range(nkb):
            score_block(1, r)

        a = outs[0] - lam * outs[1]
        ms = jnp.mean(a * a, axis=-1, keepdims=True)
        y = a * lax.rsqrt(ms + EPS) * g_ref[...] * (1.0 - lam_init)
        o_ref[0] = y.astype(BF16)

    @pl.when(lax.rem(j, 2) == 0)
    def _():
        step(sa_ref, sb_ref)

    @pl.when(lax.rem(j, 2) == 1)
    def _():
        step(sb_ref, sa_ref)


def _attention(proj3, lam, subln_g, lam_init):
    b, s, _ = proj3.shape
    nh = N_HEADS
    t = ATTN_TILE
    nq = s // t
    n_items = b * nq
    augq_np, augk_np, dg_np = _alibi_tables()
    kern = functools.partial(_attn_kernel, lam_init=lam_init, n_items=n_items)
    smem = pl.BlockSpec(memory_space=pltpu.SMEM)
    cur = lambda j: jnp.minimum(j, n_items - 1)
    prev = lambda j: jnp.maximum(j - 1, 0)
    score_scratch = pltpu.VMEM((2, s // ATTN_KEYS, t, ATTN_KEYS), F32)
    return pl.pallas_call(
        kern,
        grid=(nh, n_items + 1),
        in_specs=[smem,
                  pl.BlockSpec((1, t, V_HEAD_DIM), lambda hi, j: (cur(j) // nq, cur(j) % nq, hi)),
                  pl.BlockSpec((1, s, V_HEAD_DIM), lambda hi, j: (cur(j) // nq, 0, nh + hi)),
                  pl.BlockSpec((1, s, V_HEAD_DIM), lambda hi, j: (prev(j) // nq, 0, 2 * nh + hi)),
                  pl.BlockSpec((1, t, V_HEAD_DIM), lambda hi, j: (hi, cur(j) % nq, 0)),
                  pl.BlockSpec((1, s, V_HEAD_DIM), lambda hi, j: (hi, 0, 0)),
                  pl.BlockSpec((1, t, t), lambda hi, j: (hi, 0, 0)),
                  pl.BlockSpec((1, V_HEAD_DIM), lambda hi, j: (0, 0))],
        out_specs=pl.BlockSpec((1, t, V_HEAD_DIM), lambda hi, j: (prev(j) // nq, prev(j) % nq, hi)),
        out_shape=jax.ShapeDtypeStruct((b, s, ATTN_WIDTH), BF16),
        scratch_shapes=[pltpu.VMEM((2, s, V_HEAD_DIM), BF16), score_scratch, score_scratch,
                        pltpu.VMEM((s, 2 * V_HEAD_DIM), BF16)],
        compiler_params=_params(("arbitrary", "arbitrary"), 48),
        name="diff_attn",
    )(lam, proj3, proj3, proj3, jnp.asarray(augq_np), jnp.asarray(augk_np), jnp.asarray(dg_np), subln_g)


def _gate_kernel(u1_ref, u2_ref, u3_ref, w1_ref, w2_ref, w3_ref, b1_ref, b2_ref, b3_ref, vg_ref, x1_ref):
    s_len = u1_ref.shape[1]
    row = lax.broadcasted_iota(jnp.int32, (s_len, 1), 0)

    def conv(u_ref, w_ref, b_ref):
        u = u_ref[0].astype(F32)
        prev = jnp.where(row == 0, 0.0, pltpu.roll(u, 1, 0))
        nxt = jnp.where(row == s_len - 1, 0.0, pltpu.roll(u, s_len - 1, 0))
        w = w_ref[...]
        return b_ref[...] + prev * w[0:1] + u * w[1:2] + nxt * w[2:3]

    x1 = conv(u1_ref, w1_ref, b1_ref)
    x2 = conv(u2_ref, w2_ref, b2_ref)
    v = conv(u3_ref, w3_ref, b3_ref)
    vg_ref[0] = (v * x2).astype(BF16)
    x1_ref[0] = x1.astype(BF16)


def _gate(proj3, conv_w, conv_b, tc=256):
    b, s, _ = proj3.shape
    c = HYENA_WIDTH
    nb = c // tc
    base = 3 * ATTN_WIDTH // tc

    def uspec(part):
        return pl.BlockSpec((1, s, tc), lambda bi, ci: (bi, 0, base + part * nb + ci))

    def wspec(part):
        return pl.BlockSpec((SHORT_CONV, tc), lambda bi, ci: (0, part * nb + ci))

    def bspec(part):
        return pl.BlockSpec((1, tc), lambda bi, ci: (0, part * nb + ci))

    ospec = pl.BlockSpec((1, s, tc), lambda bi, ci: (bi, 0, ci))
    return pl.pallas_call(
        _gate_kernel,
        grid=(b, nb),
        in_specs=[uspec(0), uspec(1), uspec(2), wspec(0), wspec(1), wspec(2), bspec(0), bspec(1), bspec(2)],
        out_specs=[ospec, ospec],
        out_shape=[jax.ShapeDtypeStruct((b, s, c), BF16), jax.ShapeDtypeStruct((b, s, c), BF16)],
        compiler_params=_params(("parallel", "parallel"), 48),
        name="hyena_gate",
    )(proj3, proj3, proj3, conv_w, conv_w, conv_w, conv_b, conv_b, conv_b)


def _filter_kernel(z_ref, w1_ref, b1_ref, fr_ref, w2_ref, b2_ref, w3_ref, dl_ref, o_ref, *, tr):
    hi = lax.Precision.HIGHEST
    fr = fr_ref[...]
    h = jnp.sin(fr * (jnp.dot(z_ref[...], w1_ref[...], precision=hi, preferred_element_type=F32) + b1_ref[...]))
    h = jnp.sin(fr * (jnp.dot(h, w2_ref[...], precision=hi, preferred_element_type=F32) + b2_ref[...]))
    h = jnp.dot(h, w3_ref[...], precision=hi, preferred_element_type=F32)
    row = pl.program_id(0) * tr + lax.broadcasted_iota(jnp.int32, (tr, 1), 0)
    t = row.astype(F32) * (1.0 / (SEQ - 1))
    decay = jnp.exp(-t * dl_ref[...]) + DECAY_SHIFT
    c = HYENA_WIDTH
    o_ref[:, :c] = (h[:, :c] * decay).astype(BF16)
    o_ref[:, c:] = jnp.where(row == 0, 0.0, h[:, c:] * decay).astype(BF16)


def _filters(zpad, w1pad, b1, freq, w2, b2, w3, deltas, tr=256):
    L = SEQ
    c2 = 2 * HYENA_WIDTH
    full = lambda shape: pl.BlockSpec(shape, lambda i: (0, 0))
    return pl.pallas_call(
        functools.partial(_filter_kernel, tr=tr),
        grid=(L // tr,),
        in_specs=[pl.BlockSpec((tr, LANE), lambda i: (i, 0)),
                  full((LANE, FILTER_HIDDEN)), full((1, FILTER_HIDDEN)), full((1, FILTER_HIDDEN)),
                  full((FILTER_HIDDEN, FILTER_HIDDEN)), full((1, FILTER_HIDDEN)),
                  full((FILTER_HIDDEN, c2)), full((1, HYENA_WIDTH))],
        out_specs=pl.BlockSpec((tr, c2), lambda i: (i, 0)),
        out_shape=jax.ShapeDtypeStruct((L, c2), BF16),
        compiler_params=_params(("parallel",), 48),
        name="hyena_filters",
    )(zpad, w1pad, b1, freq, w2, b2, w3, deltas)


def _kspec_kernel(f_ref, h_ref, o_ref):
    c = HYENA_WIDTH
    t = DFT_TILE
    p = jnp.dot(f_ref[0].astype(BF16), h_ref[...], preferred_element_type=F32)
    row = pl.program_id(0) * t + lax.broadcasted_iota(jnp.int32, (t, 1), 0)
    o_ref[0] = p[:t, :c] + p[:t, c:]
    o_ref[1] = jnp.where(row == 0, p[t:, :c] + p[t:, c:], p[t:, :c] - p[t:, c:])


def _kspec(table, hcat):
    L = SEQ
    c = HYENA_WIDTH
    t = DFT_TILE
    return pl.pallas_call(
        _kspec_kernel,
        grid=(L // t,),
        in_specs=[pl.BlockSpec((1, 2 * t, L), lambda i: (i, 0, 0)),
                  pl.BlockSpec((L, 2 * c), lambda i: (0, 0), pipeline_mode=pl.Buffered(1))],
        out_specs=pl.BlockSpec((2, t, c), lambda i: (0, i, 0)),
        out_shape=jax.ShapeDtypeStruct((2, L, c), F32),
        compiler_params=_params(("arbitrary",), 56),
        name="hyena_kspec",
    )(table, hcat)


def _fwd_kernel(f_ref, vg_ref, k_ref, y_ref, fb_ref):
    t = DFT_TILE

    @pl.when(pl.program_id(1) == 0)
    def _():
        fb_ref[...] = f_ref[0].astype(BF16)

    vg = vg_ref[0]
    for r0 in range(0, t, ROW_CHUNK):
        xr = jnp.dot(fb_ref[pl.ds(r0, ROW_CHUNK), :], vg, preferred_element_type=F32)
        xi = jnp.dot(fb_ref[pl.ds(t + r0, ROW_CHUNK), :], vg, preferred_element_type=F32)
        kr = k_ref[0, pl.ds(r0, ROW_CHUNK), :]
        ki = k_ref[1, pl.ds(r0, ROW_CHUNK), :]
        row = pl.program_id(0) * t + r0 + lax.broadcasted_iota(jnp.int32, (ROW_CHUNK, 1), 0)
        first = row == 0
        yr = jnp.where(first, xr * kr, xr * kr - xi * ki)
        yi = jnp.where(first, xi * ki, xr * ki + xi * kr)
        y_ref[0, 0, pl.ds(r0, ROW_CHUNK), :] = yr.astype(BF16)
        y_ref[0, 1, pl.ds(r0, ROW_CHUNK), :] = yi.astype(BF16)


def _fwd_dft(table, vg, kspec):
    b, L, c = vg.shape
    t = DFT_TILE
    return pl.pallas_call(
        _fwd_kernel,
        grid=(L // t, b),
        in_specs=[pl.BlockSpec((1, 2 * t, L), lambda j, bi: (j, 0, 0), pipeline_mode=pl.Buffered(1)),
                  pl.BlockSpec((1, L, c), lambda j, bi: (bi, 0, 0)),
                  pl.BlockSpec((2, t, c), lambda j, bi: (0, j, 0))],
        out_specs=pl.BlockSpec((1, 2, t, c), lambda j, bi: (bi, 0, j, 0)),
        out_shape=jax.ShapeDtypeStruct((b, 2, L, c), BF16),
        scratch_shapes=[pltpu.VMEM((2 * t, L), BF16)],
        compiler_params=_params(("arbitrary", "arbitrary"), 56),
        name="hyena_fwd_dft",
    )(table, vg, kspec)


def _inv_kernel(f_ref, y_ref, vg_ref, x1_ref, d_ref, gain_ref, o_ref, g_ref):
    t = DFT_TILE
    L = SEQ

    @pl.when(pl.program_id(1) == 0)
    def _():
        row = pl.program_id(0) * t + lax.broadcasted_iota(jnp.int32, (t, 1), 0)
        col = lax.broadcasted_iota(jnp.int32, (1, L), 1)
        sign = (1 - 2 * (row & 1)).astype(F32)
        gc = f_ref[0, :t, :] * jnp.where(col == 0, 1.0 / FFT_N, 2.0 / FFT_N)
        gs = jnp.where(col == 0, sign * (1.0 / FFT_N),
                       jnp.where(row == 0, 0.0, f_ref[0, t:, :] * (2.0 / FFT_N)))
        g_ref[:, :L] = gc.astype(BF16)
        g_ref[:, L:] = gs.astype(BF16)

    for r0 in range(0, t, ROW_CHUNK):
        rows = pl.ds(r0, ROW_CHUNK)
        y = jnp.dot(g_ref[rows, :], y_ref[0], preferred_element_type=F32)
        vg = vg_ref[0, rows, :].astype(F32)
        y = (y + vg * d_ref[...]) * x1_ref[0, rows, :].astype(F32)
        ms = jnp.mean(y * y, axis=-1, keepdims=True)
        o_ref[0, rows, :] = (y * lax.rsqrt(ms + EPS) * gain_ref[...]).astype(BF16)


def _inv_dft(table, yspec, vg, x1, d_skip, gain):
    b, L, c = vg.shape
    t = DFT_TILE
    return pl.pallas_call(
        _inv_kernel,
        grid=(L // t, b),
        in_specs=[pl.BlockSpec((1, 2 * t, L), lambda j, bi: (j, 0, 0), pipeline_mode=pl.Buffered(1)),
                  pl.BlockSpec((1, 2 * L, c), lambda j, bi: (bi, 0, 0)),
                  pl.BlockSpec((1, t, c), lambda j, bi: (bi, j, 0)),
                  pl.BlockSpec((1, t, c), lambda j, bi: (bi, j, 0)),
                  pl.BlockSpec((1, c), lambda j, bi: (0, 0)),
                  pl.BlockSpec((1, c), lambda j, bi: (0, 0))],
        out_specs=pl.BlockSpec((1, t, c), lambda j, bi: (bi, j, 0)),
        out_shape=jax.ShapeDtypeStruct((b, L, c), BF16),
        scratch_shapes=[pltpu.VMEM((t, 2 * L), BF16)],
        compiler_params=_params(("arbitrary", "arbitrary"), 56),
        name="hyena_inv_dft",
    )(table, yspec, vg, x1, d_skip, gain)


def _layer_norm(y, g, b):
    mu = jnp.mean(y, axis=-1, keepdims=True)
    yc = y - mu
    var = jnp.mean(yc * yc, axis=-1, keepdims=True)
    return yc * lax.rsqrt(var + EPS) * g + b


def _outproj_kernel(att_ref, hy_ref, w_ref, x_ref, g_ref, b_ref, o_ref):
    a = ATTN_WIDTH
    for r0 in range(0, o_ref.shape[0], ROW_CHUNK):
        rows = pl.ds(r0, ROW_CHUNK)
        mix = jnp.dot(att_ref[rows, :], w_ref[:a, :], preferred_element_type=F32)
        mix = mix + jnp.dot(hy_ref[rows, :], w_ref[a:, :], preferred_element_type=F32)
        y = ALPHA * x_ref[rows, :] + mix
        o_ref[rows, :] = _layer_norm(y, g_ref[...], b_ref[...])


def _outproj(att2d, hy2d, w_bf16, x2d, g, b, tm=1024):
    m, d = x2d.shape
    return pl.pallas_call(
        _outproj_kernel,
        grid=(m // tm,),
        in_specs=[pl.BlockSpec((tm, ATTN_WIDTH), lambda i: (i, 0)),
                  pl.BlockSpec((tm, HYENA_WIDTH), lambda i: (i, 0)),
                  pl.BlockSpec((d, d), lambda i: (0, 0), pipeline_mode=pl.Buffered(1)),
                  pl.BlockSpec((tm, d), lambda i: (i, 0)),
                  pl.BlockSpec((1, d), lambda i: (0, 0)),
                  pl.BlockSpec((1, d), lambda i: (0, 0))],
        out_specs=pl.BlockSpec((tm, d), lambda i: (i, 0)),
        out_shape=jax.ShapeDtypeStruct((m, d), F32),
        compiler_params=_params(("parallel",), 56),
        name="outproj_ln",
    )(att2d, hy2d, w_bf16, x2d, g, b)


def _ffn_kernel(x_ref, w1_ref, w2_ref, g_ref, b_ref, o_ref, xb_ref):
    f = pl.program_id(1)
    last = pl.num_programs(1) - 1
    tm = o_ref.shape[0]

    def mlp(xb, base):
        h = jnp.dot(xb, w1_ref[...], preferred_element_type=F32)
        h = jnp.square(jnp.maximum(h, 0.0)).astype(BF16)
        return base + jnp.dot(h, w2_ref[...], preferred_element_type=F32)

    @pl.when(f == 0)
    def _():
        for r0 in range(0, tm, ROW_CHUNK):
            rows = pl.ds(r0, ROW_CHUNK)
            x = x_ref[rows, :]
            xb = x.astype(BF16)
            xb_ref[rows, :] = xb
            o_ref[rows, :] = mlp(xb, ALPHA * x)

    @pl.when(jnp.logical_and(f > 0, f < last))
    def _():
        o_ref[...] = mlp(xb_ref[...], o_ref[...])

    @pl.when(f == last)
    def _():
        for r0 in range(0, tm, ROW_CHUNK):
            rows = pl.ds(r0, ROW_CHUNK)
            y = mlp(xb_ref[rows, :], o_ref[rows, :])
            o_ref[rows, :] = _layer_norm(y, g_ref[...], b_ref[...])


def _ffn(x2d, w1_bf16, w2_bf16, g, b, tm=1024, tf=FFN_TILE):
    m, d = x2d.shape
    dff = w1_bf16.shape[1]
    return pl.pallas_call(
        _ffn_kernel,
        grid=(m // tm, dff // tf),
        in_specs=[pl.BlockSpec((tm, d), lambda i, j: (i, 0)),
                  pl.BlockSpec((d, tf), lambda i, j: (0, j)),
                  pl.BlockSpec((tf, d), lambda i, j: (j, 0)),
                  pl.BlockSpec((1, d), lambda i, j: (0, 0)),
                  pl.BlockSpec((1, d), lambda i, j: (0, 0))],
        out_specs=pl.BlockSpec((tm, d), lambda i, j: (i, 0)),
        out_shape=jax.ShapeDtypeStruct((m, d), F32),
        scratch_shapes=[pltpu.VMEM((tm, d), BF16)],
        compiler_params=_params(("parallel", "arbitrary"), 56),
        name="ffn_ln",
    )(x2d, w1_bf16, w2_bf16, g, b)


def kernel(x, w_in, lambda_q1, lambda_k1, lambda_q2, lambda_k2, subln_g, conv_w, conv_b, filt_w1, filt_b1,
           filt_freq, filt_w2, filt_b2, filt_w3, hyena_skip, hyena_gain, w_out, ln1_g, ln1_b, w_ff1, w_ff2,
           ln2_g, ln2_b):
    B, S, D = x.shape
    assert (B, S, D) == (BATCH, SEQ, D_MODEL)
    zpad_np, deltas_np = _filter_tables()
    dft_tab = jnp.asarray(_dft_table())
    zpad = jnp.asarray(zpad_np)
    deltas = jnp.asarray(deltas_np)
    row = lambda v: v.astype(F32).reshape(1, -1)

    x2d = x.reshape(B * S, D)
    for l in range(DEPTH):
        lam_init = 0.8 - 0.6 * math.exp(-0.3 * l)
        lam = (jnp.exp(jnp.sum(lambda_q1[l].astype(F32) * lambda_k1[l].astype(F32)))
               - jnp.exp(jnp.sum(lambda_q2[l].astype(F32) * lambda_k2[l].astype(F32)))
               + lam_init).reshape(1)

        proj = _inproj(x2d, w_in[l].astype(BF16))
        proj3 = proj.reshape(B, S, IN_COLS)

        att = _attention(proj3, lam, row(subln_g[l]), lam_init)

        vg, x1 = _gate(proj3, conv_w[l].astype(F32), row(conv_b[l]))
        w1pad = jnp.zeros((LANE, FILTER_HIDDEN), F32).at[:FILTER_EMB].set(filt_w1[l].astype(F32))
        hcat = _filters(zpad, w1pad, row(filt_b1[l]), row(filt_freq[l]), filt_w2[l].astype(F32),
                        row(filt_b2[l]), filt_w3[l].astype(F32), deltas)
        kspec = _kspec(dft_tab, hcat)
        yspec = _fwd_dft(dft_tab, vg, kspec)
        hy = _inv_dft(dft_tab, yspec.reshape(B, 2 * S, HYENA_WIDTH), vg, x1, row(hyena_skip[l]),
                      row(hyena_gain[l]))

        x2d = _outproj(att.reshape(B * S, ATTN_WIDTH), hy.reshape(B * S, HYENA_WIDTH), w_out[l].astype(BF16),
                       x2d, row(ln1_g[l]), row(ln1_b[l]))
        x2d = _ffn(x2d, w_ff1[l].astype(BF16), w_ff2[l].astype(BF16), row(ln2_g[l]), row(ln2_b[l]))
    return x2d.reshape(B, S, D)
```

```python
import functools
import math

import numpy as np
import jax
import jax.numpy as jnp
from jax import lax
from jax.experimental import pallas as pl
from jax.experimental.pallas import tpu as pltpu

D_MODEL = 2048
BATCH = 8
SEQ = 2048
DEPTH = 1
ATTN_WIDTH = D_MODEL // 2
HYENA_WIDTH = D_MODEL - ATTN_WIDTH
N_HEADS = 8
HEAD_DIM = ATTN_WIDTH // N_HEADS // 2
V_HEAD_DIM = 2 * HEAD_DIM
SHORT_CONV = 3
FILTER_EMB = 33
FILTER_HIDDEN = 64
DECAY_FAST = 0.3
DECAY_SLOW = 1.5
DECAY_TARGET = 1e-2
DECAY_SHIFT = 0.0
D_FF = 4 * D_MODEL
ALPHA = (2.0 * DEPTH) ** 0.25
EPS = 1e-5
IN_COLS = 3 * ATTN_WIDTH + 3 * HYENA_WIDTH
FFT_N = 2 * SEQ
LANE = 128
MXU_DIM = 256
DFT_TILE = 512
FFN_TILE = 512
ROW_CHUNK = 256
ATTN_TILE = 512
PV_ROWS = V_HEAD_DIM + 16
LOG2E = math.log2(math.e)
Q_PRESCALE = HEAD_DIM ** -0.5 * LOG2E

BF16 = jnp.bfloat16
F32 = jnp.float32

_MIB = 1024 * 1024


def _params(semantics, vmem_mib):
    return pltpu.CompilerParams(dimension_semantics=semantics, vmem_limit_bytes=vmem_mib * _MIB)


@functools.lru_cache(maxsize=None)
def _dft_table():
    k = np.arange(SEQ, dtype=np.int64)
    phase = (k[:, None] * k[None, :]) % FFT_N
    ang = (2.0 * np.pi / FFT_N) * phase
    c = np.cos(ang)
    s = -np.sin(ang)
    s[0, :] = np.where(k % 2 == 0, 1.0, -1.0)
    nt = SEQ // DFT_TILE
    tiled = np.concatenate([c.reshape(nt, DFT_TILE, SEQ), s.reshape(nt, DFT_TILE, SEQ)], axis=1)
    return tiled.astype(np.float32)


@functools.lru_cache(maxsize=None)
def _filter_tables():
    L = SEQ
    t = np.linspace(0.0, 1.0, L)[:, None]
    bands = (FILTER_EMB - 1) // 2
    w = 2.0 * np.pi * np.arange(L, dtype=np.float64)[:, None] / L
    f = np.linspace(1e-4, bands - 1, bands)[None, :]
    z = np.concatenate([t, np.cos(f * w), -np.sin(f * w)], axis=-1)
    zpad = np.zeros((L, LANE), np.float32)
    zpad[:, :FILTER_EMB] = z
    max_decay = math.log(DECAY_TARGET) / DECAY_FAST
    min_decay = math.log(DECAY_TARGET) / DECAY_SLOW
    deltas = np.abs(np.linspace(min_decay, max_decay, HYENA_WIDTH))[None, :].astype(np.float32)
    return zpad, deltas


def _inproj_kernel(x_ref, w_ref, cs_ref, o_ref, xb_ref):
    j = pl.program_id(1)

    @pl.when(j == 0)
    def _():
        for r0 in range(0, o_ref.shape[0], ROW_CHUNK):
            rows = pl.ds(r0, ROW_CHUNK)
            xb = x_ref[rows, :].astype(BF16)
            xb_ref[rows, :] = xb
            acc = jnp.dot(xb, w_ref[...], preferred_element_type=F32)
            o_ref[rows, :] = (acc * cs_ref[...]).astype(BF16)

    @pl.when(j > 0)
    def _():
        acc = jnp.dot(xb_ref[...], w_ref[...], preferred_element_type=F32)
        o_ref[...] = (acc * cs_ref[...]).astype(BF16)


def _inproj(x2d, w_bf16, col_scale, tm=1024, tn=1536):
    m, k = x2d.shape
    n = w_bf16.shape[1]
    return pl.pallas_call(
        _inproj_kernel,
        grid=(m // tm, n // tn),
        in_specs=[pl.BlockSpec((tm, k), lambda i, j: (i, 0)),
                  pl.BlockSpec((k, tn), lambda i, j: (0, j)),
                  pl.BlockSpec((1, tn), lambda i, j: (0, j))],
        out_specs=pl.BlockSpec((tm, tn), lambda i, j: (i, j)),
        out_shape=jax.ShapeDtypeStruct((m, n), BF16),
        scratch_shapes=[pltpu.VMEM((tm, k), BF16)],
        compiler_params=_params(("parallel", "arbitrary"), 56),
        name="inproj",
    )(x2d, w_bf16, col_scale)


@functools.lru_cache(maxsize=None)
def _alibi_tables():
    pos = np.arange(SEQ)
    hi = ((pos >> 8) << 8).astype(np.float64)
    lo = (pos & 255).astype(np.float64)
    augq = np.zeros((N_HEADS, SEQ, V_HEAD_DIM), np.float64)
    augk = np.zeros((N_HEADS, SEQ, V_HEAD_DIM), np.float64)
    t = ATTN_TILE
    ahead = np.maximum(np.arange(t)[:, None] - np.arange(t)[None, :], 0).astype(np.float64)
    dg = np.zeros((N_HEADS, t, t), np.float64)
    for h in range(N_HEADS):
        c = 2.0 ** (-8.0 * (h + 1) / N_HEADS) * LOG2E
        rest = c
        for p in range(3):
            piece = float(np.float32(rest).astype(BF16).astype(np.float64))
            rest -= piece
            for base in (0, HEAD_DIM):
                augq[h, :, base + 2 * p] = piece
                augq[h, :, base + 2 * p + 1] = piece
                augk[h, :, base + 2 * p] = hi
                augk[h, :, base + 2 * p + 1] = lo
                augq[h, :, base + 6 + 2 * p] = -hi
                augq[h, :, base + 6 + 2 * p + 1] = -lo
                augk[h, :, base + 6 + 2 * p] = piece
                augk[h, :, base + 6 + 2 * p + 1] = piece
        dg[h] = -2.0 * c * ahead
    return augq.astype(BF16), augk.astype(BF16), dg.astype(np.float32)


def _attn_kernel(lam_ref, q_ref, k_ref, v_ref, aq_ref, ak_ref, dg_ref, g_ref, o_ref,
                 kt_ref, sa_ref, sb_ref, ma_ref, mb_ref, vt_ref, ql_ref, acc_ref, *, lam_init, n_items):
    t = ATTN_TILE
    nkb = SEQ // t
    j = pl.program_id(1)
    qb = lax.rem(jnp.minimum(j, n_items - 1), nkb)
    lam = lam_ref[0]
    lane = lax.broadcasted_iota(jnp.int32, (1, V_HEAD_DIM), 1)
    first_half = lane < HEAD_DIM
    nt_dims = (((1,), (1,)), ((), ()))

    @pl.when(j == 0)
    def _():
        sb_ref[...] = jnp.zeros_like(sb_ref)
        mb_ref[...] = jnp.zeros_like(mb_ref)

    @pl.when(qb == 0)
    def _():
        k = k_ref[0]
        ak = ak_ref[0]
        kt_ref[0] = jnp.where(first_half, k, ak)
        kt_ref[1] = jnp.where(first_half, ak, k)

    @pl.when(lax.rem(jnp.maximum(j - 1, 0), nkb) == 0)
    def _():
        pad_row = lax.broadcasted_iota(jnp.int32, (PV_ROWS - V_HEAD_DIM, 1), 0)
        ones_row = jnp.where(pad_row == 0, 1.0, 0.0).astype(BF16)
        for kb in range(nkb):
            vt_ref[kb, :V_HEAD_DIM, :] = v_ref[0, kb * t:(kb + 1) * t, :].astype(F32).T.astype(BF16)
            vt_ref[kb, V_HEAD_DIM:, :] = jnp.broadcast_to(ones_row, (PV_ROWS - V_HEAD_DIM, t))

    def step(s_new, m_new, s_old, m_old):
        q = q_ref[0]
        aq = aq_ref[0]
        naq = -aq
        for c, keep in enumerate((first_half, jnp.logical_not(first_half))):
            ql_ref[c, 0] = jnp.where(keep, q, aq)
            ql_ref[c, 1] = jnp.where(keep, q, naq)
        col_max = [jnp.max(m_old[c], axis=0, keepdims=True) for c in range(2)]

        def key_block(r, first):
            wrapped = qb + r >= nkb
            kb = jnp.where(wrapped, qb + r - nkb, qb + r)
            side = 0 if first else jnp.where(wrapped, 0, 1)
            for c in range(2):
                s = lax.dot_general(kt_ref[c, pl.ds(pl.multiple_of(kb * t, t), t), :], ql_ref[c, side], nt_dims,
                                    preferred_element_type=F32)
                if first:
                    s = s + dg_ref[0]
                s_new[c, kb] = s
                bm = s[0:8]
                for i in range(8, t, 8):
                    bm = jnp.maximum(bm, s[i:i + 8])
                m_new[c] = bm if first else jnp.maximum(m_new[c], bm)
                p = jnp.exp2(s_old[c, r] - col_max[c]).astype(BF16)
                part = jnp.dot(vt_ref[r], p, preferred_element_type=F32)
                acc_ref[c] = part if first else acc_ref[c] + part

        key_block(0, True)
        for r in range(1, nkb):
            key_block(r, False)

        outs = [acc_ref[c, :V_HEAD_DIM, :] / acc_ref[c, V_HEAD_DIM:V_HEAD_DIM + 1, :] for c in range(2)]
        a = (outs[0] - lam * outs[1]).T
        ms = jnp.mean(a * a, axis=-1, keepdims=True)
        y = a * lax.rsqrt(ms + EPS) * g_ref[...] * (1.0 - lam_init)
        o_ref[0] = y.astype(BF16)

    @pl.when(lax.rem(j, 2) == 0)
    def _():
        step(sa_ref, ma_ref, sb_ref, mb_ref)

    @pl.when(lax.rem(j, 2) == 1)
    def _():
        step(sb_ref, mb_ref, sa_ref, ma_ref)


def _attention(proj3, lam, subln_g, lam_init):
    b, s, _ = proj3.shape
    nh = N_HEADS
    t = ATTN_TILE
    nq = s // t
    n_items = b * nq
    augq_np, augk_np, dg_np = _alibi_tables()
    kern = functools.partial(_attn_kernel, lam_init=lam_init, n_items=n_items)
    smem = pl.BlockSpec(memory_space=pltpu.SMEM)
    cur = lambda j: jnp.minimum(j, n_items - 1)
    prev = lambda j: jnp.maximum(j - 1, 0)
    score_scratch = pltpu.VMEM((2, nq, t, t), F32)
    max_scratch = pltpu.VMEM((2, 8, t), F32)
    return pl.pallas_call(
        kern,
        grid=(nh, n_items + 1),
        in_specs=[smem,
                  pl.BlockSpec((1, t, V_HEAD_DIM), lambda hi, j: (cur(j) // nq, cur(j) % nq, hi)),
                  pl.BlockSpec((1, s, V_HEAD_DIM), lambda hi, j: (cur(j) // nq, 0, nh + hi)),
                  pl.BlockSpec((1, s, V_HEAD_DIM), lambda hi, j: (prev(j) // nq, 0, 2 * nh + hi)),
                  pl.BlockSpec((1, t, V_HEAD_DIM), lambda hi, j: (hi, cur(j) % nq, 0)),
                  pl.BlockSpec((1, s, V_HEAD_DIM), lambda hi, j: (hi, 0, 0)),
                  pl.BlockSpec((1, t, t), lambda hi, j: (hi, 0, 0)),
                  pl.BlockSpec((1, V_HEAD_DIM), lambda hi, j: (0, 0))],
        out_specs=pl.BlockSpec((1, t, V_HEAD_DIM), lambda hi, j: (prev(j) // nq, prev(j) % nq, hi)),
        out_shape=jax.ShapeDtypeStruct((b, s, ATTN_WIDTH), BF16),
        scratch_shapes=[pltpu.VMEM((2, s, V_HEAD_DIM), BF16), score_scratch, score_scratch,
                        max_scratch, max_scratch, pltpu.VMEM((nq, PV_ROWS, t), BF16),
                        pltpu.VMEM((2, 2, t, V_HEAD_DIM), BF16), pltpu.VMEM((2, PV_ROWS, t), F32)],
        compiler_params=_params(("arbitrary", "arbitrary"), 48),
        name="diff_attn",
    )(lam, proj3, proj3, proj3, jnp.asarray(augq_np), jnp.asarray(augk_np), jnp.asarray(dg_np), subln_g)


def _gate_kernel(u1_ref, u2_ref, u3_ref, w1_ref, w2_ref, w3_ref, b1_ref, b2_ref, b3_ref, vg_ref, x1_ref):
    s_len = u1_ref.shape[1]
    row = lax.broadcasted_iota(jnp.int32, (s_len, 1), 0)

    def conv(u_ref, w_ref, b_ref):
        u = u_ref[0].astype(F32)
        prev = jnp.where(row == 0, 0.0, pltpu.roll(u, 1, 0))
        nxt = jnp.where(row == s_len - 1, 0.0, pltpu.roll(u, s_len - 1, 0))
        w = w_ref[...]
        return b_ref[...] + prev * w[0:1] + u * w[1:2] + nxt * w[2:3]

    x1 = conv(u1_ref, w1_ref, b1_ref)
    x2 = conv(u2_ref, w2_ref, b2_ref)
    v = conv(u3_ref, w3_ref, b3_ref)
    vg_ref[0] = (v * x2).astype(BF16)
    x1_ref[0] = x1.astype(BF16)


def _gate(proj3, conv_w, conv_b, tc=256):
    b, s, _ = proj3.shape
    c = HYENA_WIDTH
    nb = c // tc
    base = 3 * ATTN_WIDTH // tc

    def uspec(part):
        return pl.BlockSpec((1, s, tc), lambda bi, ci: (bi, 0, base + part * nb + ci))

    def wspec(part):
        return pl.BlockSpec((SHORT_CONV, tc), lambda bi, ci: (0, part * nb + ci))

    def bspec(part):
        return pl.BlockSpec((1, tc), lambda bi, ci: (0, part * nb + ci))

    ospec = pl.BlockSpec((1, s, tc), lambda bi, ci: (bi, 0, ci))
    return pl.pallas_call(
        _gate_kernel,
        grid=(b, nb),
        in_specs=[uspec(0), uspec(1), uspec(2), wspec(0), wspec(1), wspec(2), bspec(0), bspec(1), bspec(2)],
        out_specs=[ospec, ospec],
        out_shape=[jax.ShapeDtypeStruct((b, s, c), BF16), jax.ShapeDtypeStruct((b, s, c), BF16)],
        compiler_params=_params(("parallel", "parallel"), 48),
        name="hyena_gate",
    )(proj3, proj3, proj3, conv_w, conv_w, conv_w, conv_b, conv_b, conv_b)


def _filter_kernel(z_ref, w1_ref, b1_ref, fr_ref, w2_ref, b2_ref, w3_ref, dl_ref, o_ref, *, tr):
    hi = lax.Precision.HIGHEST
    fr = fr_ref[...]
    h = jnp.sin(fr * (jnp.dot(z_ref[...], w1_ref[...], precision=hi, preferred_element_type=F32) + b1_ref[...]))
    h = jnp.sin(fr * (jnp.dot(h, w2_ref[...], precision=hi, preferred_element_type=F32) + b2_ref[...]))
    h = jnp.dot(h, w3_ref[...], precision=hi, preferred_element_type=F32)
    row = pl.program_id(0) * tr + lax.broadcasted_iota(jnp.int32, (tr, 1), 0)
    t = row.astype(F32) * (1.0 / (SEQ - 1))
    decay = jnp.exp(-t * dl_ref[...]) + DECAY_SHIFT
    c = HYENA_WIDTH
    o_ref[:, :c] = (h[:, :c] * decay).astype(BF16)
    o_ref[:, c:] = jnp.where(row == 0, 0.0, h[:, c:] * decay).astype(BF16)


def _filters(zpad, w1pad, b1, freq, w2, b2, w3, deltas, tr=256):
    L = SEQ
    c2 = 2 * HYENA_WIDTH
    full = lambda shape: pl.BlockSpec(shape, lambda i: (0, 0))
    return pl.pallas_call(
        functools.partial(_filter_kernel, tr=tr),
        grid=(L // tr,),
        in_specs=[pl.BlockSpec((tr, LANE), lambda i: (i, 0)),
                  full((LANE, FILTER_HIDDEN)), full((1, FILTER_HIDDEN)), full((1, FILTER_HIDDEN)),
                  full((FILTER_HIDDEN, FILTER_HIDDEN)), full((1, FILTER_HIDDEN)),
                  full((FILTER_HIDDEN, c2)), full((1, HYENA_WIDTH))],
        out_specs=pl.BlockSpec((tr, c2), lambda i: (i, 0)),
        out_shape=jax.ShapeDtypeStruct((L, c2), BF16),
        compiler_params=_params(("parallel",), 48),
        name="hyena_filters",
    )(zpad, w1pad, b1, freq, w2, b2, w3, deltas)


def _kspec_kernel(f_ref, h_ref, o_ref):
    c = HYENA_WIDTH
    t = DFT_TILE
    p = jnp.dot(f_ref[0].astype(BF16), h_ref[...], preferred_element_type=F32)
    row = pl.program_id(0) * t + lax.broadcasted_iota(jnp.int32, (t, 1), 0)
    o_ref[0] = p[:t, :c] + p[:t, c:]
    o_ref[1] = jnp.where(row == 0, p[t:, :c] + p[t:, c:], p[t:, :c] - p[t:, c:])


def _kspec(table, hcat):
    L = SEQ
    c = HYENA_WIDTH
    t = DFT_TILE
    return pl.pallas_call(
        _kspec_kernel,
        grid=(L // t,),
        in_specs=[pl.BlockSpec((1, 2 * t, L), lambda i: (i, 0, 0)),
                  pl.BlockSpec((L, 2 * c), lambda i: (0, 0), pipeline_mode=pl.Buffered(1))],
        out_specs=pl.BlockSpec((2, t, c), lambda i: (0, i, 0)),
        out_shape=jax.ShapeDtypeStruct((2, L, c), F32),
        compiler_params=_params(("arbitrary",), 56),
        name="hyena_kspec",
    )(table, hcat)


def _fwd_kernel(f_ref, vg_ref, k_ref, y_ref, fb_ref):
    t = DFT_TILE

    @pl.when(pl.program_id(1) == 0)
    def _():
        fb_ref[...] = f_ref[0].astype(BF16)

    vg = vg_ref[0]
    for r0 in range(0, t, ROW_CHUNK):
        xr = jnp.dot(fb_ref[pl.ds(r0, ROW_CHUNK), :], vg, preferred_element_type=F32)
        xi = jnp.dot(fb_ref[pl.ds(t + r0, ROW_CHUNK), :], vg, preferred_element_type=F32)
        kr = k_ref[0, pl.ds(r0, ROW_CHUNK), :]
        ki = k_ref[1, pl.ds(r0, ROW_CHUNK), :]
        row = pl.program_id(0) * t + r0 + lax.broadcasted_iota(jnp.int32, (ROW_CHUNK, 1), 0)
        first = row == 0
        yr = jnp.where(first, xr * kr, xr * kr - xi * ki)
        yi = jnp.where(first, xi * ki, xr * ki + xi * kr)
        y_ref[0, 0, pl.ds(r0, ROW_CHUNK), :] = yr.astype(BF16)
        y_ref[0, 1, pl.ds(r0, ROW_CHUNK), :] = yi.astype(BF16)


def _fwd_dft(table, vg, kspec):
    b, L, c = vg.shape
    t = DFT_TILE
    return pl.pallas_call(
        _fwd_kernel,
        grid=(L // t, b),
        in_specs=[pl.BlockSpec((1, 2 * t, L), lambda j, bi: (j, 0, 0), pipeline_mode=pl.Buffered(1)),
                  pl.BlockSpec((1, L, c), lambda j, bi: (bi, 0, 0)),
                  pl.BlockSpec((2, t, c), lambda j, bi: (0, j, 0))],
        out_specs=pl.BlockSpec((1, 2, t, c), lambda j, bi: (bi, 0, j, 0)),
        out_shape=jax.ShapeDtypeStruct((b, 2, L, c), BF16),
        scratch_shapes=[pltpu.VMEM((2 * t, L), BF16)],
        compiler_params=_params(("arbitrary", "arbitrary"), 56),
        name="hyena_fwd_dft",
    )(table, vg, kspec)


def _inv_kernel(f_ref, y_ref, vg_ref, x1_ref, d_ref, gain_ref, o_ref, g_ref):
    t = DFT_TILE
    L = SEQ

    @pl.when(pl.program_id(1) == 0)
    def _():
        row = pl.program_id(0) * t + lax.broadcasted_iota(jnp.int32, (t, 1), 0)
        col = lax.broadcasted_iota(jnp.int32, (1, L), 1)
        sign = (1 - 2 * (row & 1)).astype(F32)
        gc = f_ref[0, :t, :] * jnp.where(col == 0, 1.0 / FFT_N, 2.0 / FFT_N)
        gs = jnp.where(col == 0, sign * (1.0 / FFT_N),
                       jnp.where(row == 0, 0.0, f_ref[0, t:, :] * (2.0 / FFT_N)))
        g_ref[:, :L] = gc.astype(BF16)
        g_ref[:, L:] = gs.astype(BF16)

    for r0 in range(0, t, ROW_CHUNK):
        rows = pl.ds(r0, ROW_CHUNK)
        y = jnp.dot(g_ref[rows, :], y_ref[0], preferred_element_type=F32)
        vg = vg_ref[0, rows, :].astype(F32)
        y = (y + vg * d_ref[...]) * x1_ref[0, rows, :].astype(F32)
        ms = jnp.mean(y * y, axis=-1, keepdims=True)
        o_ref[0, rows, :] = (y * lax.rsqrt(ms + EPS) * gain_ref[...]).astype(BF16)


def _inv_dft(table, yspec, vg, x1, d_skip, gain):
    b, L, c = vg.shape
    t = DFT_TILE
    return pl.pallas_call(
        _inv_kernel,
        grid=(L // t, b),
        in_specs=[pl.BlockSpec((1, 2 * t, L), lambda j, bi: (j, 0, 0), pipeline_mode=pl.Buffered(1)),
                  pl.BlockSpec((1, 2 * L, c), lambda j, bi: (bi, 0, 0)),
                  pl.BlockSpec((1, t, c), lambda j, bi: (bi, j, 0)),
                  pl.BlockSpec((1, t, c), lambda j, bi: (bi, j, 0)),
                  pl.BlockSpec((1, c), lambda j, bi: (0, 0)),
                  pl.BlockSpec((1, c), lambda j, bi: (0, 0))],
        out_specs=pl.BlockSpec((1, t, c), lambda j, bi: (bi, j, 0)),
        out_shape=jax.ShapeDtypeStruct((b, L, c), BF16),
        scratch_shapes=[pltpu.VMEM((t, 2 * L), BF16)],
        compiler_params=_params(("arbitrary", "arbitrary"), 56),
        name="hyena_inv_dft",
    )(table, yspec, vg, x1, d_skip, gain)


def _layer_norm(y, g, b):
    mu = jnp.mean(y, axis=-1, keepdims=True)
    yc = y - mu
    var = jnp.mean(yc * yc, axis=-1, keepdims=True)
    return yc * lax.rsqrt(var + EPS) * g + b


def _outproj_kernel(att_ref, hy_ref, w_ref, x_ref, g_ref, b_ref, o_ref):
    a = ATTN_WIDTH
    for r0 in range(0, o_ref.shape[0], ROW_CHUNK):
        rows = pl.ds(r0, ROW_CHUNK)
        mix = jnp.dot(att_ref[rows, :], w_ref[:a, :], preferred_element_type=F32)
        mix = mix + jnp.dot(hy_ref[rows, :], w_ref[a:, :], preferred_element_type=F32)
        y = ALPHA * x_ref[rows, :] + mix
        o_ref[rows, :] = _layer_norm(y, g_ref[...], b_ref[...])


def _outproj(att2d, hy2d, w_bf16, x2d, g, b, tm=1024):
    m, d = x2d.shape
    return pl.pallas_call(
        _outproj_kernel,
        grid=(m // tm,),
        in_specs=[pl.BlockSpec((tm, ATTN_WIDTH), lambda i: (i, 0)),
                  pl.BlockSpec((tm, HYENA_WIDTH), lambda i: (i, 0)),
                  pl.BlockSpec((d, d), lambda i: (0, 0), pipeline_mode=pl.Buffered(1)),
                  pl.BlockSpec((tm, d), lambda i: (i, 0)),
                  pl.BlockSpec((1, d), lambda i: (0, 0)),
                  pl.BlockSpec((1, d), lambda i: (0, 0))],
        out_specs=pl.BlockSpec((tm, d), lambda i: (i, 0)),
        out_shape=jax.ShapeDtypeStruct((m, d), F32),
        compiler_params=_params(("parallel",), 56),
        name="outproj_ln",
    )(att2d, hy2d, w_bf16, x2d, g, b)


def _ffn_kernel(x_ref, w1_ref, w2_ref, g_ref, b_ref, o_ref, xb_ref):
    f = pl.program_id(1)
    last = pl.num_programs(1) - 1
    tm = o_ref.shape[0]

    def mlp(xb, base):
        h = jnp.dot(xb, w1_ref[...], preferred_element_type=F32)
        h = jnp.square(jnp.maximum(h, 0.0)).astype(BF16)
        return base + jnp.dot(h, w2_ref[...], preferred_element_type=F32)

    @pl.when(f == 0)
    def _():
        for r0 in range(0, tm, ROW_CHUNK):
            rows = pl.ds(r0, ROW_CHUNK)
            x = x_ref[rows, :]
            xb = x.astype(BF16)
            xb_ref[rows, :] = xb
            o_ref[rows, :] = mlp(xb, ALPHA * x)

    @pl.when(jnp.logical_and(f > 0, f < last))
    def _():
        o_ref[...] = mlp(xb_ref[...], o_ref[...])

    @pl.when(f == last)
    def _():
        for r0 in range(0, tm, ROW_CHUNK):
            rows = pl.ds(r0, ROW_CHUNK)
            y = mlp(xb_ref[rows, :], o_ref[rows, :])
            o_ref[rows, :] = _layer_norm(y, g_ref[...], b_ref[...])


def _ffn(x2d, w1_bf16, w2_bf16, g, b, tm=1024, tf=FFN_TILE):
    m, d = x2d.shape
    dff = w1_bf16.shape[1]
    return pl.pallas_call(
        _ffn_kernel,
        grid=(m // tm, dff // tf),
        in_specs=[pl.BlockSpec((tm, d), lambda i, j: (i, 0)),
                  pl.BlockSpec((d, tf), lambda i, j: (0, j)),
                  pl.BlockSpec((tf, d), lambda i, j: (j, 0)),
                  pl.BlockSpec((1, d), lambda i, j: (0, 0)),
                  pl.BlockSpec((1, d), lambda i, j: (0, 0))],
        out_specs=pl.BlockSpec((tm, d), lambda i, j: (i, 0)),
        out_shape=jax.ShapeDtypeStruct((m, d), F32),
        scratch_shapes=[pltpu.VMEM((tm, d), BF16)],
        compiler_params=_params(("parallel", "arbitrary"), 56),
        name="ffn_ln",
    )(x2d, w1_bf16, w2_bf16, g, b)


def kernel(x, w_in, lambda_q1, lambda_k1, lambda_q2, lambda_k2, subln_g, conv_w, conv_b, filt_w1, filt_b1,
           filt_freq, filt_w2, filt_b2, filt_w3, hyena_skip, hyena_gain, w_out, ln1_g, ln1_b, w_ff1, w_ff2,
           ln2_g, ln2_b):
    B, S, D = x.shape
    assert (B, S, D) == (BATCH, SEQ, D_MODEL)
    zpad_np, deltas_np = _filter_tables()
    dft_tab = jnp.asarray(_dft_table())
    zpad = jnp.asarray(zpad_np)
    deltas = jnp.asarray(deltas_np)
    row = lambda v: v.astype(F32).reshape(1, -1)

    x2d = x.reshape(B * S, D)
    for l in range(DEPTH):
        lam_init = 0.8 - 0.6 * math.exp(-0.3 * l)
        lam = (jnp.exp(jnp.sum(lambda_q1[l].astype(F32) * lambda_k1[l].astype(F32)))
               - jnp.exp(jnp.sum(lambda_q2[l].astype(F32) * lambda_k2[l].astype(F32)))
               + lam_init).reshape(1)

        col_scale = np.ones((1, IN_COLS), np.float32)
        col_scale[:, :ATTN_WIDTH] = Q_PRESCALE
        proj = _inproj(x2d, w_in[l].astype(BF16), jnp.asarray(col_scale))
        proj3 = proj.reshape(B, S, IN_COLS)

        att = _attention(proj3, lam, row(subln_g[l]), lam_init)

        vg, x1 = _gate(proj3, conv_w[l].astype(F32), row(conv_b[l]))
        w1pad = jnp.zeros((LANE, FILTER_HIDDEN), F32).at[:FILTER_EMB].set(filt_w1[l].astype(F32))
        hcat = _filters(zpad, w1pad, row(filt_b1[l]), row(filt_freq[l]), filt_w2[l].astype(F32),
                        row(filt_b2[l]), filt_w3[l].astype(F32), deltas)
        kspec = _kspec(dft_tab, hcat)
        yspec = _fwd_dft(dft_tab, vg, kspec)
        hy = _inv_dft(dft_tab, yspec.reshape(B, 2 * S, HYENA_WIDTH), vg, x1, row(hyena_skip[l]),
                      row(hyena_gain[l]))

        x2d = _outproj(att.reshape(B * S, ATTN_WIDTH), hy.reshape(B * S, HYENA_WIDTH), w_out[l].astype(BF16),
                       x2d, row(ln1_g[l]), row(ln1_b[l]))
        x2d = _ffn(x2d, w_ff1[l].astype(BF16), w_ff2[l].astype(BF16), row(ln2_g[l]), row(ln2_b[l]))
    return x2d.reshape(B, S, D)
```

```python
import functools
import math

import numpy as np
import jax
import jax.numpy as jnp
from jax import lax
from jax.experimental import pallas as pl
from jax.experimental.pallas import tpu as pltpu

D_MODEL = 2048
BATCH = 8
SEQ = 2048
DEPTH = 1
ATTN_WIDTH = D_MODEL // 2
HYENA_WIDTH = D_MODEL - ATTN_WIDTH
N_HEADS = 8
HEAD_DIM = ATTN_WIDTH // N_HEADS // 2
V_HEAD_DIM = 2 * HEAD_DIM
SHORT_CONV = 3
FILTER_EMB = 33
FILTER_HIDDEN = 64
DECAY_FAST = 0.3
DECAY_SLOW = 1.5
DECAY_TARGET = 1e-2
DECAY_SHIFT = 0.0
D_FF = 4 * D_MODEL
ALPHA = (2.0 * DEPTH) ** 0.25
EPS = 1e-5
IN_COLS = 3 * ATTN_WIDTH + 3 * HYENA_WIDTH
FFT_N = 2 * SEQ
LANE = 128
MXU_DIM = 256
DFT_TILE = 512
FFN_TILE = 1024
ROW_CHUNK = 256
ATTN_TILE = 512
PV_ROWS = V_HEAD_DIM + 16
LOG2E = math.log2(math.e)
Q_PRESCALE = HEAD_DIM ** -0.5 * LOG2E

BF16 = jnp.bfloat16
F32 = jnp.float32

_MIB = 1024 * 1024


def _params(semantics, vmem_mib):
    return pltpu.CompilerParams(dimension_semantics=semantics, vmem_limit_bytes=vmem_mib * _MIB)


@functools.lru_cache(maxsize=None)
def _dft_table():
    k = np.arange(SEQ, dtype=np.int64)
    phase = (k[:, None] * k[None, :]) % FFT_N
    ang = (2.0 * np.pi / FFT_N) * phase
    c = np.cos(ang)
    s = -np.sin(ang)
    s[0, :] = np.where(k % 2 == 0, 1.0, -1.0)
    nt = SEQ // DFT_TILE
    tiled = np.concatenate([c.reshape(nt, DFT_TILE, SEQ), s.reshape(nt, DFT_TILE, SEQ)], axis=1)
    return tiled.astype(np.float32)


@functools.lru_cache(maxsize=None)
def _filter_tables():
    L = SEQ
    t = np.linspace(0.0, 1.0, L)[:, None]
    bands = (FILTER_EMB - 1) // 2
    w = 2.0 * np.pi * np.arange(L, dtype=np.float64)[:, None] / L
    f = np.linspace(1e-4, bands - 1, bands)[None, :]
    z = np.concatenate([t, np.cos(f * w), -np.sin(f * w)], axis=-1)
    zpad = np.zeros((L, LANE), np.float32)
    zpad[:, :FILTER_EMB] = z
    max_decay = math.log(DECAY_TARGET) / DECAY_FAST
    min_decay = math.log(DECAY_TARGET) / DECAY_SLOW
    deltas = np.abs(np.linspace(min_decay, max_decay, HYENA_WIDTH))[None, :].astype(np.float32)
    return zpad, deltas


def _inproj_kernel(x_ref, w_ref, cs_ref, wa_ref, wb_ref, wc_ref, o_ref, wa_out, wb_out, wc_out, xb_ref):
    j = pl.program_id(1)
    wa_out[...] = wa_ref[...].astype(BF16)
    wb_out[...] = wb_ref[...].astype(BF16)
    wc_out[...] = wc_ref[...].astype(BF16)

    @pl.when(j == 0)
    def _():
        for r0 in range(0, o_ref.shape[0], ROW_CHUNK):
            rows = pl.ds(r0, ROW_CHUNK)
            xb = x_ref[rows, :].astype(BF16)
            xb_ref[rows, :] = xb
            acc = jnp.dot(xb, w_ref[...], preferred_element_type=F32)
            o_ref[rows, :] = (acc * cs_ref[...]).astype(BF16)

    @pl.when(j > 0)
    def _():
        acc = jnp.dot(xb_ref[...], w_ref[...], preferred_element_type=F32)
        o_ref[...] = (acc * cs_ref[...]).astype(BF16)


def _inproj(x2d, w_bf16, col_scale, side_weights, tm=1024, tn=1536):
    m, k = x2d.shape
    n = w_bf16.shape[1]
    nj = n // tn
    steps = (m // tm) * nj

    def slab(w):
        return pl.BlockSpec((w.shape[0] // steps, w.shape[1]), lambda i, j: (i * nj + j, 0))

    side_specs = [slab(w) for w in side_weights]
    return pl.pallas_call(
        _inproj_kernel,
        grid=(m // tm, nj),
        in_specs=[pl.BlockSpec((tm, k), lambda i, j: (i, 0)),
                  pl.BlockSpec((k, tn), lambda i, j: (0, j)),
                  pl.BlockSpec((1, tn), lambda i, j: (0, j))] + side_specs,
        out_specs=[pl.BlockSpec((tm, tn), lambda i, j: (i, j))] + side_specs,
        out_shape=[jax.ShapeDtypeStruct((m, n), BF16)] + [jax.ShapeDtypeStruct(w.shape, BF16) for w in side_weights],
        scratch_shapes=[pltpu.VMEM((tm, k), BF16)],
        compiler_params=_params(("arbitrary", "arbitrary"), 56),
        name="inproj",
    )(x2d, w_bf16, col_scale, *side_weights)


@functools.lru_cache(maxsize=None)
def _alibi_tables():
    pos = np.arange(SEQ)
    hi = ((pos >> 8) << 8).astype(np.float64)
    lo = (pos & 255).astype(np.float64)
    augq = np.zeros((N_HEADS, SEQ, V_HEAD_DIM), np.float64)
    augk = np.zeros((N_HEADS, SEQ, V_HEAD_DIM), np.float64)
    t = ATTN_TILE
    ahead = np.maximum(np.arange(t)[:, None] - np.arange(t)[None, :], 0).astype(np.float64)
    dg = np.zeros((N_HEADS, t, t), np.float64)
    for h in range(N_HEADS):
        c = 2.0 ** (-8.0 * (h + 1) / N_HEADS) * LOG2E
        rest = c
        for p in range(3):
            piece = float(np.float32(rest).astype(BF16).astype(np.float64))
            rest -= piece
            for base in (0, HEAD_DIM):
                augq[h, :, base + 2 * p] = piece
                augq[h, :, base + 2 * p + 1] = piece
                augk[h, :, base + 2 * p] = hi
                augk[h, :, base + 2 * p + 1] = lo
                augq[h, :, base + 6 + 2 * p] = -hi
                augq[h, :, base + 6 + 2 * p + 1] = -lo
                augk[h, :, base + 6 + 2 * p] = piece
                augk[h, :, base + 6 + 2 * p + 1] = piece
        dg[h] = -2.0 * c * ahead
    return augq.astype(BF16), augk.astype(BF16), dg.astype(np.float32)


def _attn_kernel(lam_ref, q_ref, k_ref, v_ref, aq_ref, ak_ref, dg_ref, g_ref, o_ref,
                 kt_ref, sa_ref, sb_ref, ma_ref, mb_ref, vt_ref, ql_ref, acc_ref, *, lam_init, n_items):
    t = ATTN_TILE
    nkb = SEQ // t
    j = pl.program_id(1)
    qb = lax.rem(jnp.minimum(j, n_items - 1), nkb)
    lam = lam_ref[0]
    lane = lax.broadcasted_iota(jnp.int32, (1, V_HEAD_DIM), 1)
    first_half = lane < HEAD_DIM
    nt_dims = (((1,), (1,)), ((), ()))

    @pl.when(j == 0)
    def _():
        sb_ref[...] = jnp.zeros_like(sb_ref)
        mb_ref[...] = jnp.zeros_like(mb_ref)

    @pl.when(qb == 0)
    def _():
        k = k_ref[0]
        ak = ak_ref[0]
        kt_ref[0] = jnp.where(first_half, k, ak)
        kt_ref[1] = jnp.where(first_half, ak, k)

    @pl.when(lax.rem(jnp.maximum(j - 1, 0), nkb) == 0)
    def _():
        pad_row = lax.broadcasted_iota(jnp.int32, (PV_ROWS - V_HEAD_DIM, 1), 0)
        ones_row = jnp.where(pad_row == 0, 1.0, 0.0).astype(BF16)
        for kb in range(nkb):
            vt_ref[kb, :V_HEAD_DIM, :] = v_ref[0, kb * t:(kb + 1) * t, :].astype(F32).T.astype(BF16)
            vt_ref[kb, V_HEAD_DIM:, :] = jnp.broadcast_to(ones_row, (PV_ROWS - V_HEAD_DIM, t))

    def step(s_new, m_new, s_old, m_old):
        q = q_ref[0]
        aq = aq_ref[0]
        naq = -aq
        for c, keep in enumerate((first_half, jnp.logical_not(first_half))):
            ql_ref[c, 0] = jnp.where(keep, q, aq)
            ql_ref[c, 1] = jnp.where(keep, q, naq)
        col_max = [jnp.max(m_old[c], axis=0, keepdims=True) for c in range(2)]

        def key_block(r, first):
            wrapped = qb + r >= nkb
            kb = jnp.where(wrapped, qb + r - nkb, qb + r)
            side = 0 if first else jnp.where(wrapped, 0, 1)
            for c in range(2):
                s = lax.dot_general(kt_ref[c, pl.ds(pl.multiple_of(kb * t, t), t), :], ql_ref[c, side], nt_dims,
                                    preferred_element_type=F32)
                if first:
                    s = s + dg_ref[0]
                s_new[c, kb] = s
                bm = s[0:8]
                for i in range(8, t, 8):
                    bm = jnp.maximum(bm, s[i:i + 8])
                m_new[c] = bm if first else jnp.maximum(m_new[c], bm)
                p = jnp.exp2(s_old[c, r] - col_max[c]).astype(BF16)
                part = jnp.dot(vt_ref[r], p, preferred_element_type=F32)
                acc_ref[c] = part if first else acc_ref[c] + part

        key_block(0, True)
        for r in range(1, nkb):
            key_block(r, False)

        outs = [acc_ref[c, :V_HEAD_DIM, :] / acc_ref[c, V_HEAD_DIM:V_HEAD_DIM + 1, :] for c in range(2)]
        a = (outs[0] - lam * outs[1]).T
        ms = jnp.mean(a * a, axis=-1, keepdims=True)
        y = a * lax.rsqrt(ms + EPS) * g_ref[...] * (1.0 - lam_init)
        o_ref[0] = y.astype(BF16)

    @pl.when(lax.rem(j, 2) == 0)
    def _():
        step(sa_ref, ma_ref, sb_ref, mb_ref)

    @pl.when(lax.rem(j, 2) == 1)
    def _():
        step(sb_ref, mb_ref, sa_ref, ma_ref)


def _attention(proj3, lam, subln_g, lam_init):
    b, s, _ = proj3.shape
    nh = N_HEADS
    t = ATTN_TILE
    nq = s // t
    n_items = b * nq
    augq_np, augk_np, dg_np = _alibi_tables()
    kern = functools.partial(_attn_kernel, lam_init=lam_init, n_items=n_items)
    smem = pl.BlockSpec(memory_space=pltpu.SMEM)
    cur = lambda j: jnp.minimum(j, n_items - 1)
    prev = lambda j: jnp.maximum(j - 1, 0)
    score_scratch = pltpu.VMEM((2, nq, t, t), F32)
    max_scratch = pltpu.VMEM((2, 8, t), F32)
    return pl.pallas_call(
        kern,
        grid=(nh, n_items + 1),
        in_specs=[smem,
                  pl.BlockSpec((1, t, V_HEAD_DIM), lambda hi, j: (cur(j) // nq, cur(j) % nq, hi)),
                  pl.BlockSpec((1, s, V_HEAD_DIM), lambda hi, j: (cur(j) // nq, 0, nh + hi)),
                  pl.BlockSpec((1, s, V_HEAD_DIM), lambda hi, j: (prev(j) // nq, 0, 2 * nh + hi)),
                  pl.BlockSpec((1, t, V_HEAD_DIM), lambda hi, j: (hi, cur(j) % nq, 0)),
                  pl.BlockSpec((1, s, V_HEAD_DIM), lambda hi, j: (hi, 0, 0)),
                  pl.BlockSpec((1, t, t), lambda hi, j: (hi, 0, 0)),
                  pl.BlockSpec((1, V_HEAD_DIM), lambda hi, j: (0, 0))],
        out_specs=pl.BlockSpec((1, t, V_HEAD_DIM), lambda hi, j: (prev(j) // nq, prev(j) % nq, hi)),
        out_shape=jax.ShapeDtypeStruct((b, s, ATTN_WIDTH), BF16),
        scratch_shapes=[pltpu.VMEM((2, s, V_HEAD_DIM), BF16), score_scratch, score_scratch,
                        max_scratch, max_scratch, pltpu.VMEM((nq, PV_ROWS, t), BF16),
                        pltpu.VMEM((2, 2, t, V_HEAD_DIM), BF16), pltpu.VMEM((2, PV_ROWS, t), F32)],
        compiler_params=_params(("arbitrary", "arbitrary"), 48),
        name="diff_attn",
    )(lam, proj3, proj3, proj3, jnp.asarray(augq_np), jnp.asarray(augk_np), jnp.asarray(dg_np), subln_g)


def _gate_kernel(u1_ref, u2_ref, u3_ref, w1_ref, w2_ref, w3_ref, b1_ref, b2_ref, b3_ref, vg_ref, x1_ref):
    s_len = u1_ref.shape[1]
    row = lax.broadcasted_iota(jnp.int32, (s_len, 1), 0)

    def conv(u_ref, w_ref, b_ref):
        u = u_ref[0].astype(F32)
        prev = jnp.where(row == 0, 0.0, pltpu.roll(u, 1, 0))
        nxt = jnp.where(row == s_len - 1, 0.0, pltpu.roll(u, s_len - 1, 0))
        w = w_ref[...]
        return b_ref[...] + prev * w[0:1] + u * w[1:2] + nxt * w[2:3]

    x1 = conv(u1_ref, w1_ref, b1_ref)
    x2 = conv(u2_ref, w2_ref, b2_ref)
    v = conv(u3_ref, w3_ref, b3_ref)
    vg_ref[0] = (v * x2).astype(BF16)
    x1_ref[0] = x1.astype(BF16)


def _gate(proj3, conv_w, conv_b, tc=256):
    b, s, _ = proj3.shape
    c = HYENA_WIDTH
    nb = c // tc
    base = 3 * ATTN_WIDTH // tc

    def uspec(part):
        return pl.BlockSpec((1, s, tc), lambda bi, ci: (bi, 0, base + part * nb + ci))

    def wspec(part):
        return pl.BlockSpec((SHORT_CONV, tc), lambda bi, ci: (0, part * nb + ci))

    def bspec(part):
        return pl.BlockSpec((1, tc), lambda bi, ci: (0, part * nb + ci))

    ospec = pl.BlockSpec((1, s, tc), lambda bi, ci: (bi, 0, ci))
    return pl.pallas_call(
        _gate_kernel,
        grid=(b, nb),
        in_specs=[uspec(0), uspec(1), uspec(2), wspec(0), wspec(1), wspec(2), bspec(0), bspec(1), bspec(2)],
        out_specs=[ospec, ospec],
        out_shape=[jax.ShapeDtypeStruct((b, s, c), BF16), jax.ShapeDtypeStruct((b, s, c), BF16)],
        compiler_params=_params(("parallel", "parallel"), 48),
        name="hyena_gate",
    )(proj3, proj3, proj3, conv_w, conv_w, conv_w, conv_b, conv_b, conv_b)


def _filter_kernel(z_ref, w1_ref, b1_ref, fr_ref, w2_ref, b2_ref, w3_ref, dl_ref, o_ref, *, tr):
    hi = lax.Precision.HIGHEST
    fr = fr_ref[...]
    h = jnp.sin(fr * (jnp.dot(z_ref[...], w1_ref[...], precision=hi, preferred_element_type=F32) + b1_ref[...]))
    h = jnp.sin(fr * (jnp.dot(h, w2_ref[...], precision=hi, preferred_element_type=F32) + b2_ref[...]))
    h = jnp.dot(h.astype(BF16), w3_ref[...].astype(BF16), preferred_element_type=F32)
    row = pl.program_id(0) * tr + lax.broadcasted_iota(jnp.int32, (tr, 1), 0)
    t = row.astype(F32) * (1.0 / (SEQ - 1))
    decay = jnp.exp(-t * dl_ref[...]) + DECAY_SHIFT
    c = HYENA_WIDTH
    o_ref[:, :c] = (h[:, :c] * decay).astype(BF16)
    o_ref[:, c:] = jnp.where(row == 0, 0.0, h[:, c:] * decay).astype(BF16)


def _filters(zpad, w1pad, b1, freq, w2, b2, w3, deltas, tr=256):
    L = SEQ
    c2 = 2 * HYENA_WIDTH
    full = lambda shape: pl.BlockSpec(shape, lambda i: (0, 0))
    return pl.pallas_call(
        functools.partial(_filter_kernel, tr=tr),
        grid=(L // tr,),
        in_specs=[pl.BlockSpec((tr, LANE), lambda i: (i, 0)),
                  full((LANE, FILTER_HIDDEN)), full((1, FILTER_HIDDEN)), full((1, FILTER_HIDDEN)),
                  full((FILTER_HIDDEN, FILTER_HIDDEN)), full((1, FILTER_HIDDEN)),
                  full((FILTER_HIDDEN, c2)), full((1, HYENA_WIDTH))],
        out_specs=pl.BlockSpec((tr, c2), lambda i: (i, 0)),
        out_shape=jax.ShapeDtypeStruct((L, c2), BF16),
        compiler_params=_params(("parallel",), 48),
        name="hyena_filters",
    )(zpad, w1pad, b1, freq, w2, b2, w3, deltas)


def _kspec_kernel(f_ref, h_ref, o_ref):
    c = HYENA_WIDTH
    t = DFT_TILE
    p = jnp.dot(f_ref[0].astype(BF16), h_ref[...], preferred_element_type=F32)
    row = pl.program_id(0) * t + lax.broadcasted_iota(jnp.int32, (t, 1), 0)
    o_ref[0] = p[:t, :c] + p[:t, c:]
    o_ref[1] = jnp.where(row == 0, p[t:, :c] + p[t:, c:], p[t:, :c] - p[t:, c:])


def _kspec(table, hcat):
    L = SEQ
    c = HYENA_WIDTH
    t = DFT_TILE
    return pl.pallas_call(
        _kspec_kernel,
        grid=(L // t,),
        in_specs=[pl.BlockSpec((1, 2 * t, L), lambda i: (i, 0, 0)),
                  pl.BlockSpec((L, 2 * c), lambda i: (0, 0), pipeline_mode=pl.Buffered(1))],
        out_specs=pl.BlockSpec((2, t, c), lambda i: (0, i, 0)),
        out_shape=jax.ShapeDtypeStruct((2, L, c), F32),
        compiler_params=_params(("arbitrary",), 56),
        name="hyena_kspec",
    )(table, hcat)


def _fwd_kernel(f_ref, vg_ref, k_ref, y_ref, fb_ref):
    t = DFT_TILE

    @pl.when(pl.program_id(1) == 0)
    def _():
        fb_ref[...] = f_ref[0].astype(BF16)

    vg = vg_ref[0]
    for r0 in range(0, t, ROW_CHUNK):
        xr = jnp.dot(fb_ref[pl.ds(r0, ROW_CHUNK), :], vg, preferred_element_type=F32)
        xi = jnp.dot(fb_ref[pl.ds(t + r0, ROW_CHUNK), :], vg, preferred_element_type=F32)
        kr = k_ref[0, pl.ds(r0, ROW_CHUNK), :]
        ki = k_ref[1, pl.ds(r0, ROW_CHUNK), :]
        row = pl.program_id(0) * t + r0 + lax.broadcasted_iota(jnp.int32, (ROW_CHUNK, 1), 0)
        first = row == 0
        yr = jnp.where(first, xr * kr, xr * kr - xi * ki)
        yi = jnp.where(first, xi * ki, xr * ki + xi * kr)
        y_ref[0, 0, pl.ds(r0, ROW_CHUNK), :] = yr.astype(BF16)
        y_ref[0, 1, pl.ds(r0, ROW_CHUNK), :] = yi.astype(BF16)


def _fwd_dft(table, vg, kspec):
    b, L, c = vg.shape
    t = DFT_TILE
    return pl.pallas_call(
        _fwd_kernel,
        grid=(L // t, b),
        in_specs=[pl.BlockSpec((1, 2 * t, L), lambda j, bi: (j, 0, 0), pipeline_mode=pl.Buffered(1)),
                  pl.BlockSpec((1, L, c), lambda j, bi: (bi, 0, 0)),
                  pl.BlockSpec((2, t, c), lambda j, bi: (0, j, 0))],
        out_specs=pl.BlockSpec((1, 2, t, c), lambda j, bi: (bi, 0, j, 0)),
        out_shape=jax.ShapeDtypeStruct((b, 2, L, c), BF16),
        scratch_shapes=[pltpu.VMEM((2 * t, L), BF16)],
        compiler_params=_params(("arbitrary", "arbitrary"), 56),
        name="hyena_fwd_dft",
    )(table, vg, kspec)


def _inv_kernel(f_ref, y_ref, vg_ref, x1_ref, d_ref, gain_ref, o_ref, g_ref):
    t = DFT_TILE
    L = SEQ

    @pl.when(pl.program_id(1) == 0)
    def _():
        row = pl.program_id(0) * t + lax.broadcasted_iota(jnp.int32, (t, 1), 0)
        col = lax.broadcasted_iota(jnp.int32, (1, L), 1)
        sign = (1 - 2 * (row & 1)).astype(F32)
        gc = f_ref[0, :t, :] * jnp.where(col == 0, 1.0 / FFT_N, 2.0 / FFT_N)
        gs = jnp.where(col == 0, sign * (1.0 / FFT_N),
                       jnp.where(row == 0, 0.0, f_ref[0, t:, :] * (2.0 / FFT_N)))
        g_ref[:, :L] = gc.astype(BF16)
        g_ref[:, L:] = gs.astype(BF16)

    for r0 in range(0, t, ROW_CHUNK):
        rows = pl.ds(r0, ROW_CHUNK)
        y = jnp.dot(g_ref[rows, :], y_ref[0], preferred_element_type=F32)
        vg = vg_ref[0, rows, :].astype(F32)
        y = (y + vg * d_ref[...]) * x1_ref[0, rows, :].astype(F32)
        ms = jnp.mean(y * y, axis=-1, keepdims=True)
        o_ref[0, rows, :] = (y * lax.rsqrt(ms + EPS) * gain_ref[...]).astype(BF16)


def _inv_dft(table, yspec, vg, x1, d_skip, gain):
    b, L, c = vg.shape
    t = DFT_TILE
    return pl.pallas_call(
        _inv_kernel,
        grid=(L // t, b),
        in_specs=[pl.BlockSpec((1, 2 * t, L), lambda j, bi: (j, 0, 0), pipeline_mode=pl.Buffered(1)),
                  pl.BlockSpec((1, 2 * L, c), lambda j, bi: (bi, 0, 0)),
                  pl.BlockSpec((1, t, c), lambda j, bi: (bi, j, 0)),
                  pl.BlockSpec((1, t, c), lambda j, bi: (bi, j, 0)),
                  pl.BlockSpec((1, c), lambda j, bi: (0, 0)),
                  pl.BlockSpec((1, c), lambda j, bi: (0, 0))],
        out_specs=pl.BlockSpec((1, t, c), lambda j, bi: (bi, j, 0)),
        out_shape=jax.ShapeDtypeStruct((b, L, c), BF16),
        scratch_shapes=[pltpu.VMEM((t, 2 * L), BF16)],
        compiler_params=_params(("arbitrary", "arbitrary"), 56),
        name="hyena_inv_dft",
    )(table, yspec, vg, x1, d_skip, gain)


def _layer_norm(y, g, b):
    mu = jnp.mean(y, axis=-1, keepdims=True)
    yc = y - mu
    var = jnp.mean(yc * yc, axis=-1, keepdims=True)
    return yc * lax.rsqrt(var + EPS) * g + b


def _outproj_kernel(att_ref, hy_ref, w_ref, x_ref, g_ref, b_ref, o_ref):
    a = ATTN_WIDTH
    for r0 in range(0, o_ref.shape[0], ROW_CHUNK):
        rows = pl.ds(r0, ROW_CHUNK)
        mix = jnp.dot(att_ref[rows, :], w_ref[:a, :], preferred_element_type=F32)
        mix = mix + jnp.dot(hy_ref[rows, :], w_ref[a:, :], preferred_element_type=F32)
        y = ALPHA * x_ref[rows, :] + mix
        o_ref[rows, :] = _layer_norm(y, g_ref[...], b_ref[...])


def _outproj(att2d, hy2d, w_bf16, x2d, g, b, tm=1024):
    m, d = x2d.shape
    return pl.pallas_call(
        _outproj_kernel,
        grid=(m // tm,),
        in_specs=[pl.BlockSpec((tm, ATTN_WIDTH), lambda i: (i, 0)),
                  pl.BlockSpec((tm, HYENA_WIDTH), lambda i: (i, 0)),
                  pl.BlockSpec((d, d), lambda i: (0, 0), pipeline_mode=pl.Buffered(1)),
                  pl.BlockSpec((tm, d), lambda i: (i, 0)),
                  pl.BlockSpec((1, d), lambda i: (0, 0)),
                  pl.BlockSpec((1, d), lambda i: (0, 0))],
        out_specs=pl.BlockSpec((tm, d), lambda i: (i, 0)),
        out_shape=jax.ShapeDtypeStruct((m, d), F32),
        compiler_params=_params(("parallel",), 56),
        name="outproj_ln",
    )(att2d, hy2d, w_bf16, x2d, g, b)


def _ffn_kernel(x_ref, w1_ref, w2_ref, g_ref, b_ref, o_ref, xb_ref):
    f = pl.program_id(1)
    last = pl.num_programs(1) - 1
    tm = o_ref.shape[0]

    def mlp(xb, base):
        h = jnp.dot(xb, w1_ref[...], preferred_element_type=F32)
        h = jnp.square(jnp.maximum(h, 0.0)).astype(BF16)
        return base + jnp.dot(h, w2_ref[...], preferred_element_type=F32)

    @pl.when(f == 0)
    def _():
        for r0 in range(0, tm, ROW_CHUNK):
            rows = pl.ds(r0, ROW_CHUNK)
            x = x_ref[rows, :]
            xb = x.astype(BF16)
            xb_ref[rows, :] = xb
            o_ref[rows, :] = mlp(xb, ALPHA * x)

    @pl.when(jnp.logical_and(f > 0, f < last))
    def _():
        for r0 in range(0, tm, 2 * ROW_CHUNK):
            rows = pl.ds(r0, 2 * ROW_CHUNK)
            o_ref[rows, :] = mlp(xb_ref[rows, :], o_ref[rows, :])

    @pl.when(f == last)
    def _():
        for r0 in range(0, tm, ROW_CHUNK):
            rows = pl.ds(r0, ROW_CHUNK)
            y = mlp(xb_ref[rows, :], o_ref[rows, :])
            o_ref[rows, :] = _layer_norm(y, g_ref[...], b_ref[...])


def _ffn(x2d, w1_bf16, w2_bf16, g, b, tm=1024, tf=FFN_TILE):
    m, d = x2d.shape
    dff = w1_bf16.shape[1]
    return pl.pallas_call(
        _ffn_kernel,
        grid=(m // tm, dff // tf),
        in_specs=[pl.BlockSpec((tm, d), lambda i, j: (i, 0)),
                  pl.BlockSpec((d, tf), lambda i, j: (0, j)),
                  pl.BlockSpec((tf, d), lambda i, j: (j, 0)),
                  pl.BlockSpec((1, d), lambda i, j: (0, 0)),
                  pl.BlockSpec((1, d), lambda i, j: (0, 0))],
        out_specs=pl.BlockSpec((tm, d), lambda i, j: (i, 0)),
        out_shape=jax.ShapeDtypeStruct((m, d), F32),
        scratch_shapes=[pltpu.VMEM((tm, d), BF16)],
        compiler_params=_params(("parallel", "arbitrary"), 58),
        name="ffn_ln",
    )(x2d, w1_bf16, w2_bf16, g, b)


def kernel(x, w_in, lambda_q1, lambda_k1, lambda_q2, lambda_k2, subln_g, conv_w, conv_b, filt_w1, filt_b1,
           filt_freq, filt_w2, filt_b2, filt_w3, hyena_skip, hyena_gain, w_out, ln1_g, ln1_b, w_ff1, w_ff2,
           ln2_g, ln2_b):
    B, S, D = x.shape
    assert (B, S, D) == (BATCH, SEQ, D_MODEL)
    zpad_np, deltas_np = _filter_tables()
    dft_tab = jnp.asarray(_dft_table())
    zpad = jnp.asarray(zpad_np)
    deltas = jnp.asarray(deltas_np)
    row = lambda v: v.astype(F32).reshape(1, -1)

    x2d = x.reshape(B * S, D)
    for l in range(DEPTH):
        lam_init = 0.8 - 0.6 * math.exp(-0.3 * l)
        lam = (jnp.exp(jnp.sum(lambda_q1[l].astype(F32) * lambda_k1[l].astype(F32)))
               - jnp.exp(jnp.sum(lambda_q2[l].astype(F32) * lambda_k2[l].astype(F32)))
               + lam_init).reshape(1)

        col_scale = np.ones((1, IN_COLS), np.float32)
        col_scale[:, :ATTN_WIDTH] = Q_PRESCALE
        proj, w_out_b, w_ff1_b, w_ff2_b = _inproj(
            x2d, w_in[l].astype(BF16), jnp.asarray(col_scale),
            [w_out[l].astype(F32), w_ff1[l].astype(F32), w_ff2[l].astype(F32)])
        proj3 = proj.reshape(B, S, IN_COLS)

        att = _attention(proj3, lam, row(subln_g[l]), lam_init)

        vg, x1 = _gate(proj3, conv_w[l].astype(F32), row(conv_b[l]))
        w1pad = jnp.zeros((LANE, FILTER_HIDDEN), F32).at[:FILTER_EMB].set(filt_w1[l].astype(F32))
        hcat = _filters(zpad, w1pad, row(filt_b1[l]), row(filt_freq[l]), filt_w2[l].astype(F32),
                        row(filt_b2[l]), filt_w3[l].astype(F32), deltas)
        kspec = _kspec(dft_tab, hcat)
        yspec = _fwd_dft(dft_tab, vg, kspec)
        hy = _inv_dft(dft_tab, yspec.reshape(B, 2 * S, HYENA_WIDTH), vg, x1, row(hyena_skip[l]),
                      row(hyena_gain[l]))

        x2d = _outproj(att.reshape(B * S, ATTN_WIDTH), hy.reshape(B * S, HYENA_WIDTH), w_out_b,
                       x2d, row(ln1_g[l]), row(ln1_b[l]))
        x2d = _ffn(x2d, w_ff1_b, w_ff2_b, row(ln2_g[l]), row(ln2_b[l]))
    return x2d.reshape(B, S, D)
```

```python
import functools
import math

import numpy as np
import jax
import jax.numpy as jnp
from jax import lax
from jax.experimental import pallas as pl
from jax.experimental.pallas import tpu as pltpu

D_MODEL = 2048
BATCH = 8
SEQ = 2048
DEPTH = 1
ATTN_WIDTH = D_MODEL // 2
HYENA_WIDTH = D_MODEL - ATTN_WIDTH
N_HEADS = 8
HEAD_DIM = ATTN_WIDTH // N_HEADS // 2
V_HEAD_DIM = 2 * HEAD_DIM
SHORT_CONV = 3
FILTER_EMB = 33
FILTER_HIDDEN = 64
DECAY_FAST = 0.3
DECAY_SLOW = 1.5
DECAY_TARGET = 1e-2
DECAY_SHIFT = 0.0
D_FF = 4 * D_MODEL
ALPHA = (2.0 * DEPTH) ** 0.25
EPS = 1e-5
IN_COLS = 3 * ATTN_WIDTH + 3 * HYENA_WIDTH
FFT_N = 2 * SEQ
LANE = 128
MXU_DIM = 256
DFT_TILE = 512
FFN_TILE = 1024
ROW_CHUNK = 256
ATTN_TILE = 512
PV_ROWS = V_HEAD_DIM + 16
LOG2E = math.log2(math.e)
Q_PRESCALE = HEAD_DIM ** -0.5 * LOG2E

BF16 = jnp.bfloat16
F32 = jnp.float32

_MIB = 1024 * 1024


def _params(semantics, vmem_mib):
    return pltpu.CompilerParams(dimension_semantics=semantics, vmem_limit_bytes=vmem_mib * _MIB)


@functools.lru_cache(maxsize=None)
def _dft_table():
    k = np.arange(SEQ, dtype=np.int64)
    phase = (k[:, None] * k[None, :]) % FFT_N
    ang = (2.0 * np.pi / FFT_N) * phase
    c = np.cos(ang)
    s = -np.sin(ang)
    s[0, :] = np.where(k % 2 == 0, 1.0, -1.0)
    nt = SEQ // DFT_TILE
    tiled = np.concatenate([c.reshape(nt, DFT_TILE, SEQ), s.reshape(nt, DFT_TILE, SEQ)], axis=1)
    return tiled.astype(np.float32)


@functools.lru_cache(maxsize=None)
def _filter_tables():
    L = SEQ
    t = np.linspace(0.0, 1.0, L)[:, None]
    bands = (FILTER_EMB - 1) // 2
    w = 2.0 * np.pi * np.arange(L, dtype=np.float64)[:, None] / L
    f = np.linspace(1e-4, bands - 1, bands)[None, :]
    z = np.concatenate([t, np.cos(f * w), -np.sin(f * w)], axis=-1)
    zpad = np.zeros((L, LANE), np.float32)
    zpad[:, :FILTER_EMB] = z
    max_decay = math.log(DECAY_TARGET) / DECAY_FAST
    min_decay = math.log(DECAY_TARGET) / DECAY_SLOW
    deltas = np.abs(np.linspace(min_decay, max_decay, HYENA_WIDTH))[None, :].astype(np.float32)
    return zpad, deltas


def _inproj_kernel(x_ref, w_ref, cs_ref, wa_ref, wb_ref, wc_ref, o_ref, wa_out, wb_out, wc_out, xb_ref):
    j = pl.program_id(1)
    wa_out[...] = wa_ref[...].astype(BF16)
    wb_out[...] = wb_ref[...].astype(BF16)
    wc_out[...] = wc_ref[...].astype(BF16)

    @pl.when(j == 0)
    def _():
        for r0 in range(0, o_ref.shape[0], ROW_CHUNK):
            rows = pl.ds(r0, ROW_CHUNK)
            xb = x_ref[rows, :].astype(BF16)
            xb_ref[rows, :] = xb
            acc = jnp.dot(xb, w_ref[...], preferred_element_type=F32)
            o_ref[rows, :] = (acc * cs_ref[...]).astype(BF16)

    @pl.when(j > 0)
    def _():
        acc = jnp.dot(xb_ref[...], w_ref[...], preferred_element_type=F32)
        o_ref[...] = (acc * cs_ref[...]).astype(BF16)


def _inproj(x2d, w_bf16, col_scale, side_weights, tm=1024, tn=1536):
    m, k = x2d.shape
    n = w_bf16.shape[1]
    nj = n // tn
    steps = (m // tm) * nj

    def slab(w):
        return pl.BlockSpec((w.shape[0] // steps, w.shape[1]), lambda i, j: (i * nj + j, 0))

    side_specs = [slab(w) for w in side_weights]
    return pl.pallas_call(
        _inproj_kernel,
        grid=(m // tm, nj),
        in_specs=[pl.BlockSpec((tm, k), lambda i, j: (i, 0)),
                  pl.BlockSpec((k, tn), lambda i, j: (0, j)),
                  pl.BlockSpec((1, tn), lambda i, j: (0, j))] + side_specs,
        out_specs=[pl.BlockSpec((tm, tn), lambda i, j: (i, j))] + side_specs,
        out_shape=[jax.ShapeDtypeStruct((m, n), BF16)] + [jax.ShapeDtypeStruct(w.shape, BF16) for w in side_weights],
        scratch_shapes=[pltpu.VMEM((tm, k), BF16)],
        compiler_params=_params(("arbitrary", "arbitrary"), 56),
        name="inproj",
    )(x2d, w_bf16, col_scale, *side_weights)


@functools.lru_cache(maxsize=None)
def _alibi_tables():
    pos = np.arange(SEQ)
    hi = ((pos >> 8) << 8).astype(np.float64)
    lo = (pos & 255).astype(np.float64)
    augq = np.zeros((N_HEADS, SEQ, V_HEAD_DIM), np.float64)
    augk = np.zeros((N_HEADS, SEQ, V_HEAD_DIM), np.float64)
    t = ATTN_TILE
    ahead = np.maximum(np.arange(t)[:, None] - np.arange(t)[None, :], 0).astype(np.float64)
    dg = np.zeros((N_HEADS, t, t), np.float64)
    for h in range(N_HEADS):
        c = 2.0 ** (-8.0 * (h + 1) / N_HEADS) * LOG2E
        rest = c
        for p in range(3):
            piece = float(np.float32(rest).astype(BF16).astype(np.float64))
            rest -= piece
            for base in (0, HEAD_DIM):
                augq[h, :, base + 2 * p] = piece
                augq[h, :, base + 2 * p + 1] = piece
                augk[h, :, base + 2 * p] = hi
                augk[h, :, base + 2 * p + 1] = lo
                augq[h, :, base + 6 + 2 * p] = -hi
                augq[h, :, base + 6 + 2 * p + 1] = -lo
                augk[h, :, base + 6 + 2 * p] = piece
                augk[h, :, base + 6 + 2 * p + 1] = piece
        dg[h] = -2.0 * c * ahead
    return augq.astype(BF16), augk.astype(BF16), dg.astype(np.float32)


def _attn_kernel(lam_ref, q_ref, k_ref, v_ref, aq_ref, ak_ref, dg_ref, g_ref, o_ref,
                 kt_ref, sa_ref, sb_ref, ma_ref, mb_ref, vt_ref, ql_ref, acc_ref, *, lam_init, n_items):
    t = ATTN_TILE
    nkb = SEQ // t
    j = pl.program_id(1)
    qb = lax.rem(jnp.minimum(j, n_items - 1), nkb)
    lam = lam_ref[0]
    lane = lax.broadcasted_iota(jnp.int32, (1, V_HEAD_DIM), 1)
    first_half = lane < HEAD_DIM
    nt_dims = (((1,), (1,)), ((), ()))

    @pl.when(j == 0)
    def _():
        sb_ref[...] = jnp.zeros_like(sb_ref)
        mb_ref[...] = jnp.zeros_like(mb_ref)

    @pl.when(qb == 0)
    def _():
        k = k_ref[0]
        ak = ak_ref[0]
        kt_ref[0] = jnp.where(first_half, k, ak)
        kt_ref[1] = jnp.where(first_half, ak, k)

    @pl.when(lax.rem(jnp.maximum(j - 1, 0), nkb) == 0)
    def _():
        pad_row = lax.broadcasted_iota(jnp.int32, (PV_ROWS - V_HEAD_DIM, 1), 0)
        ones_row = jnp.where(pad_row == 0, 1.0, 0.0).astype(BF16)
        for kb in range(nkb):
            vt_ref[kb, :V_HEAD_DIM, :] = v_ref[0, kb * t:(kb + 1) * t, :].astype(F32).T.astype(BF16)
            vt_ref[kb, V_HEAD_DIM:, :] = jnp.broadcast_to(ones_row, (PV_ROWS - V_HEAD_DIM, t))

    def step(s_new, m_new, s_old, m_old):
        q = q_ref[0]
        aq = aq_ref[0]
        naq = -aq
        for c, keep in enumerate((first_half, jnp.logical_not(first_half))):
            ql_ref[c, 0] = jnp.where(keep, q, aq)
            ql_ref[c, 1] = jnp.where(keep, q, naq)
        col_max = [jnp.max(m_old[c], axis=0, keepdims=True) for c in range(2)]

        def key_block(r, first):
            wrapped = qb + r >= nkb
            kb = jnp.where(wrapped, qb + r - nkb, qb + r)
            side = 0 if first else jnp.where(wrapped, 0, 1)
            for c in range(2):
                p = jnp.exp2(s_old[c, r] - col_max[c]).astype(BF16)
                part = jnp.dot(vt_ref[r], p, preferred_element_type=F32)
                acc_ref[c] = part if first else acc_ref[c] + part
                s = lax.dot_general(kt_ref[c, pl.ds(pl.multiple_of(kb * t, t), t), :], ql_ref[c, side], nt_dims,
                                    preferred_element_type=F32)
                if first:
                    s = s + dg_ref[0]
                s_new[c, kb] = s
                bm = s[0:8]
                for i in range(8, t, 8):
                    bm = jnp.maximum(bm, s[i:i + 8])
                m_new[c] = bm if first else jnp.maximum(m_new[c], bm)

        key_block(0, True)
        for r in range(1, nkb):
            key_block(r, False)

        outs = [acc_ref[c, :V_HEAD_DIM, :] / acc_ref[c, V_HEAD_DIM:V_HEAD_DIM + 1, :] for c in range(2)]
        a = (outs[0] - lam * outs[1]).T
        ms = jnp.mean(a * a, axis=-1, keepdims=True)
        y = a * lax.rsqrt(ms + EPS) * g_ref[...] * (1.0 - lam_init)
        o_ref[0] = y.astype(BF16)

    @pl.when(lax.rem(j, 2) == 0)
    def _():
        step(sa_ref, ma_ref, sb_ref, mb_ref)

    @pl.when(lax.rem(j, 2) == 1)
    def _():
        step(sb_ref, mb_ref, sa_ref, ma_ref)


def _attention(proj3, lam, subln_g, lam_init):
    b, s, _ = proj3.shape
    nh = N_HEADS
    t = ATTN_TILE
    nq = s // t
    n_items = b * nq
    augq_np, augk_np, dg_np = _alibi_tables()
    kern = functools.partial(_attn_kernel, lam_init=lam_init, n_items=n_items)
    smem = pl.BlockSpec(memory_space=pltpu.SMEM)
    cur = lambda j: jnp.minimum(j, n_items - 1)
    prev = lambda j: jnp.maximum(j - 1, 0)
    score_scratch = pltpu.VMEM((2, nq, t, t), F32)
    max_scratch = pltpu.VMEM((2, 8, t), F32)
    return pl.pallas_call(
        kern,
        grid=(nh, n_items + 1),
        in_specs=[smem,
                  pl.BlockSpec((1, t, V_HEAD_DIM), lambda hi, j: (cur(j) // nq, cur(j) % nq, hi)),
                  pl.BlockSpec((1, s, V_HEAD_DIM), lambda hi, j: (cur(j) // nq, 0, nh + hi)),
                  pl.BlockSpec((1, s, V_HEAD_DIM), lambda hi, j: (prev(j) // nq, 0, 2 * nh + hi)),
                  pl.BlockSpec((1, t, V_HEAD_DIM), lambda hi, j: (hi, cur(j) % nq, 0)),
                  pl.BlockSpec((1, s, V_HEAD_DIM), lambda hi, j: (hi, 0, 0)),
                  pl.BlockSpec((1, t, t), lambda hi, j: (hi, 0, 0)),
                  pl.BlockSpec((1, V_HEAD_DIM), lambda hi, j: (0, 0))],
        out_specs=pl.BlockSpec((1, t, V_HEAD_DIM), lambda hi, j: (prev(j) // nq, prev(j) % nq, hi)),
        out_shape=jax.ShapeDtypeStruct((b, s, ATTN_WIDTH), BF16),
        scratch_shapes=[pltpu.VMEM((2, s, V_HEAD_DIM), BF16), score_scratch, score_scratch,
                        max_scratch, max_scratch, pltpu.VMEM((nq, PV_ROWS, t), BF16),
                        pltpu.VMEM((2, 2, t, V_HEAD_DIM), BF16), pltpu.VMEM((2, PV_ROWS, t), F32)],
        compiler_params=_params(("arbitrary", "arbitrary"), 48),
        name="diff_attn",
    )(lam, proj3, proj3, proj3, jnp.asarray(augq_np), jnp.asarray(augk_np), jnp.asarray(dg_np), subln_g)


def _gate_kernel(u1_ref, u2_ref, u3_ref, w1_ref, w2_ref, w3_ref, b1_ref, b2_ref, b3_ref, vg_ref, x1_ref):
    s_len = u1_ref.shape[1]
    row = lax.broadcasted_iota(jnp.int32, (s_len, 1), 0)

    def conv(u_ref, w_ref, b_ref):
        u = u_ref[0].astype(F32)
        prev = jnp.where(row == 0, 0.0, pltpu.roll(u, 1, 0))
        nxt = jnp.where(row == s_len - 1, 0.0, pltpu.roll(u, s_len - 1, 0))
        w = w_ref[...]
        return b_ref[...] + prev * w[0:1] + u * w[1:2] + nxt * w[2:3]

    x1 = conv(u1_ref, w1_ref, b1_ref)
    x2 = conv(u2_ref, w2_ref, b2_ref)
    v = conv(u3_ref, w3_ref, b3_ref)
    vg_ref[0] = (v * x2).astype(BF16)
    x1_ref[0] = x1.astype(BF16)


def _gate(proj3, conv_w, conv_b, tc=256):
    b, s, _ = proj3.shape
    c = HYENA_WIDTH
    nb = c // tc
    base = 3 * ATTN_WIDTH // tc

    def uspec(part):
        return pl.BlockSpec((1, s, tc), lambda bi, ci: (bi, 0, base + part * nb + ci))

    def wspec(part):
        return pl.BlockSpec((SHORT_CONV, tc), lambda bi, ci: (0, part * nb + ci))

    def bspec(part):
        return pl.BlockSpec((1, tc), lambda bi, ci: (0, part * nb + ci))

    ospec = pl.BlockSpec((1, s, tc), lambda bi, ci: (bi, 0, ci))
    return pl.pallas_call(
        _gate_kernel,
        grid=(b, nb),
        in_specs=[uspec(0), uspec(1), uspec(2), wspec(0), wspec(1), wspec(2), bspec(0), bspec(1), bspec(2)],
        out_specs=[ospec, ospec],
        out_shape=[jax.ShapeDtypeStruct((b, s, c), BF16), jax.ShapeDtypeStruct((b, s, c), BF16)],
        compiler_params=_params(("parallel", "parallel"), 48),
        name="hyena_gate",
    )(proj3, proj3, proj3, conv_w, conv_w, conv_w, conv_b, conv_b, conv_b)


def _filter_kernel(z_ref, w1_ref, b1_ref, fr_ref, w2_ref, b2_ref, w3_ref, dl_ref, o_ref, *, tr):
    hi = lax.Precision.HIGHEST
    fr = fr_ref[...]
    h = jnp.sin(fr * (jnp.dot(z_ref[...], w1_ref[...], precision=hi, preferred_element_type=F32) + b1_ref[...]))
    h = jnp.sin(fr * (jnp.dot(h, w2_ref[...], precision=hi, preferred_element_type=F32) + b2_ref[...]))
    h = jnp.dot(h.astype(BF16), w3_ref[...].astype(BF16), preferred_element_type=F32)
    row = pl.program_id(0) * tr + lax.broadcasted_iota(jnp.int32, (tr, 1), 0)
    t = row.astype(F32) * (1.0 / (SEQ - 1))
    decay = jnp.exp(-t * dl_ref[...]) + DECAY_SHIFT
    c = HYENA_WIDTH
    o_ref[:, :c] = (h[:, :c] * decay).astype(BF16)
    o_ref[:, c:] = jnp.where(row == 0, 0.0, h[:, c:] * decay).astype(BF16)


def _filters(zpad, w1pad, b1, freq, w2, b2, w3, deltas, tr=256):
    L = SEQ
    c2 = 2 * HYENA_WIDTH
    full = lambda shape: pl.BlockSpec(shape, lambda i: (0, 0))
    return pl.pallas_call(
        functools.partial(_filter_kernel, tr=tr),
        grid=(L // tr,),
        in_specs=[pl.BlockSpec((tr, LANE), lambda i: (i, 0)),
                  full((LANE, FILTER_HIDDEN)), full((1, FILTER_HIDDEN)), full((1, FILTER_HIDDEN)),
                  full((FILTER_HIDDEN, FILTER_HIDDEN)), full((1, FILTER_HIDDEN)),
                  full((FILTER_HIDDEN, c2)), full((1, HYENA_WIDTH))],
        out_specs=pl.BlockSpec((tr, c2), lambda i: (i, 0)),
        out_shape=jax.ShapeDtypeStruct((L, c2), BF16),
        compiler_params=_params(("parallel",), 48),
        name="hyena_filters",
    )(zpad, w1pad, b1, freq, w2, b2, w3, deltas)


def _kspec_kernel(f_ref, h_ref, o_ref, fb_ref, g_ref):
    c = HYENA_WIDTH
    t = DFT_TILE
    L = SEQ
    fb = f_ref[0].astype(BF16)
    fb_ref[0] = fb
    p = jnp.dot(fb, h_ref[...], preferred_element_type=F32)
    row = pl.program_id(0) * t + lax.broadcasted_iota(jnp.int32, (t, 1), 0)
    o_ref[0] = p[:t, :c] + p[:t, c:]
    o_ref[1] = jnp.where(row == 0, p[t:, :c] + p[t:, c:], p[t:, :c] - p[t:, c:])
    col = lax.broadcasted_iota(jnp.int32, (1, L), 1)
    sign = (1 - 2 * (row & 1)).astype(F32)
    gc = f_ref[0, :t, :] * jnp.where(col == 0, 1.0 / FFT_N, 2.0 / FFT_N)
    gs = jnp.where(col == 0, sign * (1.0 / FFT_N),
                   jnp.where(row == 0, 0.0, f_ref[0, t:, :] * (2.0 / FFT_N)))
    g_ref[:, :L] = gc.astype(BF16)
    g_ref[:, L:] = gs.astype(BF16)


def _kspec(table, hcat):
    L = SEQ
    c = HYENA_WIDTH
    t = DFT_TILE
    return pl.pallas_call(
        _kspec_kernel,
        grid=(L // t,),
        in_specs=[pl.BlockSpec((1, 2 * t, L), lambda i: (i, 0, 0)),
                  pl.BlockSpec((L, 2 * c), lambda i: (0, 0), pipeline_mode=pl.Buffered(1))],
        out_specs=[pl.BlockSpec((2, t, c), lambda i: (0, i, 0)),
                   pl.BlockSpec((1, 2 * t, L), lambda i: (i, 0, 0)),
                   pl.BlockSpec((t, 2 * L), lambda i: (i, 0))],
        out_shape=[jax.ShapeDtypeStruct((2, L, c), F32),
                   jax.ShapeDtypeStruct(table.shape, BF16),
                   jax.ShapeDtypeStruct((L, 2 * L), BF16)],
        compiler_params=_params(("arbitrary",), 56),
        name="hyena_kspec",
    )(table, hcat)


def _fwd_kernel(f_ref, vg_ref, k_ref, y_ref):
    t = DFT_TILE
    for bb in range(vg_ref.shape[0]):
        vg = vg_ref[bb]
        for r0 in range(0, t, ROW_CHUNK):
            xr = jnp.dot(f_ref[0, pl.ds(r0, ROW_CHUNK), :], vg, preferred_element_type=F32)
            xi = jnp.dot(f_ref[0, pl.ds(t + r0, ROW_CHUNK), :], vg, preferred_element_type=F32)
            kr = k_ref[0, pl.ds(r0, ROW_CHUNK), :]
            ki = k_ref[1, pl.ds(r0, ROW_CHUNK), :]
            row = pl.program_id(0) * t + r0 + lax.broadcasted_iota(jnp.int32, (ROW_CHUNK, 1), 0)
            first = row == 0
            yr = jnp.where(first, xr * kr, xr * kr - xi * ki)
            yi = jnp.where(first, xi * ki, xr * ki + xi * kr)
            y_ref[bb, 0, pl.ds(r0, ROW_CHUNK), :] = yr.astype(BF16)
            y_ref[bb, 1, pl.ds(r0, ROW_CHUNK), :] = yi.astype(BF16)


def _fwd_dft(table_bf16, vg, kspec, nb=2):
    b, L, c = vg.shape
    t = DFT_TILE
    return pl.pallas_call(
        _fwd_kernel,
        grid=(L // t, b // nb),
        in_specs=[pl.BlockSpec((1, 2 * t, L), lambda j, bi: (j, 0, 0)),
                  pl.BlockSpec((nb, L, c), lambda j, bi: (bi, 0, 0)),
                  pl.BlockSpec((2, t, c), lambda j, bi: (0, j, 0))],
        out_specs=pl.BlockSpec((nb, 2, t, c), lambda j, bi: (bi, 0, j, 0)),
        out_shape=jax.ShapeDtypeStruct((b, 2, L, c), BF16),
        compiler_params=_params(("arbitrary", "arbitrary"), 56),
        name="hyena_fwd_dft",
    )(table_bf16, vg, kspec)


def _inv_kernel(g_ref, y_ref, vg_ref, x1_ref, d_ref, gain_ref, o_ref):
    for r0 in range(0, o_ref.shape[1], ROW_CHUNK):
        rows = pl.ds(r0, ROW_CHUNK)
        y = jnp.dot(g_ref[rows, :], y_ref[0], preferred_element_type=F32)
        vg = vg_ref[0, rows, :].astype(F32)
        y = (y + vg * d_ref[...]) * x1_ref[0, rows, :].astype(F32)
        ms = jnp.mean(y * y, axis=-1, keepdims=True)
        o_ref[0, rows, :] = (y * lax.rsqrt(ms + EPS) * gain_ref[...]).astype(BF16)


def _inv_dft(inv_weights, yspec, vg, x1, d_skip, gain, tt=1024):
    b, L, c = vg.shape
    return pl.pallas_call(
        _inv_kernel,
        grid=(L // tt, b),
        in_specs=[pl.BlockSpec((tt, 2 * L), lambda j, bi: (j, 0)),
                  pl.BlockSpec((1, 2 * L, c), lambda j, bi: (bi, 0, 0)),
                  pl.BlockSpec((1, tt, c), lambda j, bi: (bi, j, 0)),
                  pl.BlockSpec((1, tt, c), lambda j, bi: (bi, j, 0)),
                  pl.BlockSpec((1, c), lambda j, bi: (0, 0)),
                  pl.BlockSpec((1, c), lambda j, bi: (0, 0))],
        out_specs=pl.BlockSpec((1, tt, c), lambda j, bi: (bi, j, 0)),
        out_shape=jax.ShapeDtypeStruct((b, L, c), BF16),
        compiler_params=_params(("arbitrary", "arbitrary"), 56),
        name="hyena_inv_dft",
    )(inv_weights, yspec, vg, x1, d_skip, gain)


def _layer_norm(y, g, b):
    mu = jnp.mean(y, axis=-1, keepdims=True)
    yc = y - mu
    var = jnp.mean(yc * yc, axis=-1, keepdims=True)
    return yc * lax.rsqrt(var + EPS) * g + b


def _outproj_kernel(att_ref, hy_ref, w_ref, x_ref, g_ref, b_ref, o_ref):
    a = ATTN_WIDTH
    for r0 in range(0, o_ref.shape[0], ROW_CHUNK):
        rows = pl.ds(r0, ROW_CHUNK)
        mix = jnp.dot(att_ref[rows, :], w_ref[:a, :], preferred_element_type=F32)
        mix = mix + jnp.dot(hy_ref[rows, :], w_ref[a:, :], preferred_element_type=F32)
        y = ALPHA * x_ref[rows, :] + mix
        o_ref[rows, :] = _layer_norm(y, g_ref[...], b_ref[...])


def _outproj(att2d, hy2d, w_bf16, x2d, g, b, tm=1024):
    m, d = x2d.shape
    return pl.pallas_call(
        _outproj_kernel,
        grid=(m // tm,),
        in_specs=[pl.BlockSpec((tm, ATTN_WIDTH), lambda i: (i, 0)),
                  pl.BlockSpec((tm, HYENA_WIDTH), lambda i: (i, 0)),
                  pl.BlockSpec((d, d), lambda i: (0, 0), pipeline_mode=pl.Buffered(1)),
                  pl.BlockSpec((tm, d), lambda i: (i, 0)),
                  pl.BlockSpec((1, d), lambda i: (0, 0)),
                  pl.BlockSpec((1, d), lambda i: (0, 0))],
        out_specs=pl.BlockSpec((tm, d), lambda i: (i, 0)),
        out_shape=jax.ShapeDtypeStruct((m, d), F32),
        compiler_params=_params(("parallel",), 56),
        name="outproj_ln",
    )(att2d, hy2d, w_bf16, x2d, g, b)


def _ffn_kernel(x_ref, w1_ref, w2_ref, g_ref, b_ref, o_ref, xb_ref):
    f = pl.program_id(1)
    last = pl.num_programs(1) - 1
    tm = o_ref.shape[0]

    def mlp(xb, base):
        h = jnp.dot(xb, w1_ref[...], preferred_element_type=F32)
        h = jnp.square(jnp.maximum(h, 0.0)).astype(BF16)
        return base + jnp.dot(h, w2_ref[...], preferred_element_type=F32)

    @pl.when(f == 0)
    def _():
        for r0 in range(0, tm, ROW_CHUNK):
            rows = pl.ds(r0, ROW_CHUNK)
            x = x_ref[rows, :]
            xb = x.astype(BF16)
            xb_ref[rows, :] = xb
            o_ref[rows, :] = mlp(xb, ALPHA * x)

    @pl.when(jnp.logical_and(f > 0, f < last))
    def _():
        for r0 in range(0, tm, 2 * ROW_CHUNK):
            rows = pl.ds(r0, 2 * ROW_CHUNK)
            o_ref[rows, :] = mlp(xb_ref[rows, :], o_ref[rows, :])

    @pl.when(f == last)
    def _():
        for r0 in range(0, tm, ROW_CHUNK):
            rows = pl.ds(r0, ROW_CHUNK)
            y = mlp(xb_ref[rows, :], o_ref[rows, :])
            o_ref[rows, :] = _layer_norm(y, g_ref[...], b_ref[...])


def _ffn(x2d, w1_bf16, w2_bf16, g, b, tm=1024, tf=FFN_TILE):
    m, d = x2d.shape
    dff = w1_bf16.shape[1]
    return pl.pallas_call(
        _ffn_kernel,
        grid=(m // tm, dff // tf),
        in_specs=[pl.BlockSpec((tm, d), lambda i, j: (i, 0)),
                  pl.BlockSpec((d, tf), lambda i, j: (0, j)),
                  pl.BlockSpec((tf, d), lambda i, j: (j, 0)),
                  pl.BlockSpec((1, d), lambda i, j: (0, 0)),
                  pl.BlockSpec((1, d), lambda i, j: (0, 0))],
        out_specs=pl.BlockSpec((tm, d), lambda i, j: (i, 0)),
        out_shape=jax.ShapeDtypeStruct((m, d), F32),
        scratch_shapes=[pltpu.VMEM((tm, d), BF16)],
        compiler_params=_params(("parallel", "arbitrary"), 58),
        name="ffn_ln",
    )(x2d, w1_bf16, w2_bf16, g, b)


def kernel(x, w_in, lambda_q1, lambda_k1, lambda_q2, lambda_k2, subln_g, conv_w, conv_b, filt_w1, filt_b1,
           filt_freq, filt_w2, filt_b2, filt_w3, hyena_skip, hyena_gain, w_out, ln1_g, ln1_b, w_ff1, w_ff2,
           ln2_g, ln2_b):
    B, S, D = x.shape
    assert (B, S, D) == (BATCH, SEQ, D_MODEL)
    zpad_np, deltas_np = _filter_tables()
    dft_tab = jnp.asarray(_dft_table())
    zpad = jnp.asarray(zpad_np)
    deltas = jnp.asarray(deltas_np)
    row = lambda v: v.astype(F32).reshape(1, -1)

    x2d = x.reshape(B * S, D)
    for l in range(DEPTH):
        lam_init = 0.8 - 0.6 * math.exp(-0.3 * l)
        lam = (jnp.exp(jnp.sum(lambda_q1[l].astype(F32) * lambda_k1[l].astype(F32)))
               - jnp.exp(jnp.sum(lambda_q2[l].astype(F32) * lambda_k2[l].astype(F32)))
               + lam_init).reshape(1)

        col_scale = np.ones((1, IN_COLS), np.float32)
        col_scale[:, :ATTN_WIDTH] = Q_PRESCALE
        proj, w_out_b, w_ff1_b, w_ff2_b = _inproj(
            x2d, w_in[l].astype(BF16), jnp.asarray(col_scale),
            [w_out[l].astype(F32), w_ff1[l].astype(F32), w_ff2[l].astype(F32)])
        proj3 = proj.reshape(B, S, IN_COLS)

        att = _attention(proj3, lam, row(subln_g[l]), lam_init)

        vg, x1 = _gate(proj3, conv_w[l].astype(F32), row(conv_b[l]))
        w1pad = jnp.zeros((LANE, FILTER_HIDDEN), F32).at[:FILTER_EMB].set(filt_w1[l].astype(F32))
        hcat = _filters(zpad, w1pad, row(filt_b1[l]), row(filt_freq[l]), filt_w2[l].astype(F32),
                        row(filt_b2[l]), filt_w3[l].astype(F32), deltas)
        kspec, dft_tab_b, inv_weights = _kspec(dft_tab, hcat)
        yspec = _fwd_dft(dft_tab_b, vg, kspec)
        hy = _inv_dft(inv_weights, yspec.reshape(B, 2 * S, HYENA_WIDTH), vg, x1, row(hyena_skip[l]),
                      row(hyena_gain[l]))

        x2d = _outproj(att.reshape(B * S, ATTN_WIDTH), hy.reshape(B * S, HYENA_WIDTH), w_out_b,
                       x2d, row(ln1_g[l]), row(ln1_b[l]))
        x2d = _ffn(x2d, w_ff1_b, w_ff2_b, row(ln2_g[l]), row(ln2_b[l]))
    return x2d.reshape(B, S, D)
```

```python
import functools
import math

import numpy as np
import jax
import jax.numpy as jnp
from jax import lax
from jax.experimental import pallas as pl
from jax.experimental.pallas import tpu as pltpu

D_MODEL = 2048
BATCH = 8
SEQ = 2048
DEPTH = 1
ATTN_WIDTH = D_MODEL // 2
HYENA_WIDTH = D_MODEL - ATTN_WIDTH
N_HEADS = 8
HEAD_DIM = ATTN_WIDTH // N_HEADS // 2
V_HEAD_DIM = 2 * HEAD_DIM
SHORT_CONV = 3
FILTER_EMB = 33
FILTER_HIDDEN = 64
DECAY_FAST = 0.3
DECAY_SLOW = 1.5
DECAY_TARGET = 1e-2
DECAY_SHIFT = 0.0
D_FF = 4 * D_MODEL
ALPHA = (2.0 * DEPTH) ** 0.25
EPS = 1e-5
IN_COLS = 3 * ATTN_WIDTH + 3 * HYENA_WIDTH
FFT_N = 2 * SEQ
LANE = 128
MXU_DIM = 256
DFT_TILE = 512
FFN_TILE = 1024
ROW_CHUNK = 256
ATTN_TILE = 1024
ATTN_KEYS = 512
PV_ROWS = V_HEAD_DIM + 16
LOG2E = math.log2(math.e)
Q_PRESCALE = HEAD_DIM ** -0.5 * LOG2E

BF16 = jnp.bfloat16
F32 = jnp.float32

_MIB = 1024 * 1024


def _params(semantics, vmem_mib):
    return pltpu.CompilerParams(dimension_semantics=semantics, vmem_limit_bytes=vmem_mib * _MIB)


@functools.lru_cache(maxsize=None)
def _dft_table():
    k = np.arange(SEQ, dtype=np.int64)
    phase = (k[:, None] * k[None, :]) % FFT_N
    ang = (2.0 * np.pi / FFT_N) * phase
    c = np.cos(ang)
    s = -np.sin(ang)
    s[0, :] = np.where(k % 2 == 0, 1.0, -1.0)
    nt = SEQ // DFT_TILE
    tiled = np.concatenate([c.reshape(nt, DFT_TILE, SEQ), s.reshape(nt, DFT_TILE, SEQ)], axis=1)
    return tiled.astype(np.float32)


@functools.lru_cache(maxsize=None)
def _filter_tables():
    L = SEQ
    t = np.linspace(0.0, 1.0, L)[:, None]
    bands = (FILTER_EMB - 1) // 2
    w = 2.0 * np.pi * np.arange(L, dtype=np.float64)[:, None] / L
    f = np.linspace(1e-4, bands - 1, bands)[None, :]
    z = np.concatenate([t, np.cos(f * w), -np.sin(f * w)], axis=-1)
    zpad = np.zeros((L, LANE), np.float32)
    zpad[:, :FILTER_EMB] = z
    max_decay = math.log(DECAY_TARGET) / DECAY_FAST
    min_decay = math.log(DECAY_TARGET) / DECAY_SLOW
    deltas = np.abs(np.linspace(min_decay, max_decay, HYENA_WIDTH))[None, :].astype(np.float32)
    return zpad, deltas


def _inproj_kernel(x_ref, w_ref, cs_ref, wa_ref, wb_ref, wc_ref, o_ref, wa_out, wb_out, wc_out, xb_ref):
    j = pl.program_id(1)
    wa_out[...] = wa_ref[...].astype(BF16)
    wb_out[...] = wb_ref[...].astype(BF16)
    wc_out[...] = wc_ref[...].astype(BF16)

    @pl.when(j == 0)
    def _():
        for r0 in range(0, o_ref.shape[0], ROW_CHUNK):
            rows = pl.ds(r0, ROW_CHUNK)
            xb = x_ref[rows, :].astype(BF16)
            xb_ref[rows, :] = xb
            acc = jnp.dot(xb, w_ref[...], preferred_element_type=F32)
            o_ref[rows, :] = (acc * cs_ref[...]).astype(BF16)

    @pl.when(j > 0)
    def _():
        acc = jnp.dot(xb_ref[...], w_ref[...], preferred_element_type=F32)
        o_ref[...] = (acc * cs_ref[...]).astype(BF16)


def _inproj(x2d, w_bf16, col_scale, side_weights, tm=1024, tn=1536):
    m, k = x2d.shape
    n = w_bf16.shape[1]
    nj = n // tn
    steps = (m // tm) * nj

    def slab(w):
        return pl.BlockSpec((w.shape[0] // steps, w.shape[1]), lambda i, j: (i * nj + j, 0))

    side_specs = [slab(w) for w in side_weights]
    return pl.pallas_call(
        _inproj_kernel,
        grid=(m // tm, nj),
        in_specs=[pl.BlockSpec((tm, k), lambda i, j: (i, 0)),
                  pl.BlockSpec((k, tn), lambda i, j: (0, j)),
                  pl.BlockSpec((1, tn), lambda i, j: (0, j))] + side_specs,
        out_specs=[pl.BlockSpec((tm, tn), lambda i, j: (i, j))] + side_specs,
        out_shape=[jax.ShapeDtypeStruct((m, n), BF16)] + [jax.ShapeDtypeStruct(w.shape, BF16) for w in side_weights],
        scratch_shapes=[pltpu.VMEM((tm, k), BF16)],
        compiler_params=_params(("arbitrary", "arbitrary"), 56),
        name="inproj",
    )(x2d, w_bf16, col_scale, *side_weights)


@functools.lru_cache(maxsize=None)
def _alibi_tables():
    pos = np.arange(SEQ)
    hi = ((pos >> 8) << 8).astype(np.float64)
    lo = (pos & 255).astype(np.float64)
    augq = np.zeros((N_HEADS, SEQ, V_HEAD_DIM), np.float64)
    augk = np.zeros((N_HEADS, SEQ, V_HEAD_DIM), np.float64)
    t = ATTN_TILE
    ahead = np.maximum(np.arange(t)[:, None] - np.arange(t)[None, :], 0).astype(np.float64)
    dg = np.zeros((N_HEADS, t, t), np.float64)
    for h in range(N_HEADS):
        c = 2.0 ** (-8.0 * (h + 1) / N_HEADS) * LOG2E
        rest = c
        for p in range(3):
            piece = float(np.float32(rest).astype(BF16).astype(np.float64))
            rest -= piece
            for base in (0, HEAD_DIM):
                augq[h, :, base + 2 * p] = piece
                augq[h, :, base + 2 * p + 1] = piece
                augk[h, :, base + 2 * p] = hi
                augk[h, :, base + 2 * p + 1] = lo
                augq[h, :, base + 6 + 2 * p] = -hi
                augq[h, :, base + 6 + 2 * p + 1] = -lo
                augk[h, :, base + 6 + 2 * p] = piece
                augk[h, :, base + 6 + 2 * p + 1] = piece
        dg[h] = -2.0 * c * ahead
    return augq.astype(BF16), augk.astype(BF16), dg.astype(np.float32)


def _attn_kernel(lam_ref, q_ref, k_ref, v_ref, aq_ref, ak_ref, dg_ref, g_ref, o_ref,
                 kt_ref, sa_ref, sb_ref, ma_ref, mb_ref, vt_ref, ql_ref, acc_ref, *, lam_init, n_items):
    t = ATTN_TILE
    tk = ATTN_KEYS
    nkb = SEQ // tk
    nq = SEQ // t
    diag_blocks = t // tk
    j = pl.program_id(1)
    qb = lax.rem(jnp.minimum(j, n_items - 1), nq)
    lam = lam_ref[0]
    lane = lax.broadcasted_iota(jnp.int32, (1, V_HEAD_DIM), 1)
    first_half = lane < HEAD_DIM
    nt_dims = (((1,), (1,)), ((), ()))

    @pl.when(j == 0)
    def _():
        sb_ref[...] = jnp.zeros_like(sb_ref)
        mb_ref[...] = jnp.zeros_like(mb_ref)

    @pl.when(qb == 0)
    def _():
        k = k_ref[0]
        ak = ak_ref[0]
        kt_ref[0] = jnp.where(first_half, k, ak)
        kt_ref[1] = jnp.where(first_half, ak, k)

    @pl.when(lax.rem(jnp.maximum(j - 1, 0), nq) == 0)
    def _():
        pad_row = lax.broadcasted_iota(jnp.int32, (PV_ROWS - V_HEAD_DIM, 1), 0)
        ones_row = jnp.where(pad_row == 0, 1.0, 0.0).astype(BF16)
        for kb in range(nkb):
            vt_ref[kb, :V_HEAD_DIM, :] = v_ref[0, kb * tk:(kb + 1) * tk, :].astype(F32).T.astype(BF16)
            vt_ref[kb, V_HEAD_DIM:, :] = jnp.broadcast_to(ones_row, (PV_ROWS - V_HEAD_DIM, tk))

    def step(s_new, m_new, s_old, m_old):
        q = q_ref[0]
        aq = aq_ref[0]
        naq = -aq
        for c, keep in enumerate((first_half, jnp.logical_not(first_half))):
            ql_ref[c, 0] = jnp.where(keep, q, aq)
            ql_ref[c, 1] = jnp.where(keep, q, naq)
        col_max = [jnp.max(m_old[c], axis=0, keepdims=True) for c in range(2)]

        def key_block(r):
            first = r == 0
            shared = r < diag_blocks
            start = qb * diag_blocks + r
            wrapped = start >= nkb
            kb = jnp.where(wrapped, start - nkb, start)
            side = 0 if shared else jnp.where(wrapped, 0, 1)
            for c in range(2):
                p = jnp.exp2(s_old[c, r] - col_max[c]).astype(BF16)
                part = jnp.dot(vt_ref[r], p, preferred_element_type=F32)
                acc_ref[c] = part if first else acc_ref[c] + part
                s = lax.dot_general(kt_ref[c, pl.ds(pl.multiple_of(kb * tk, tk), tk), :], ql_ref[c, side], nt_dims,
                                    preferred_element_type=F32)
                if shared:
                    s = s + dg_ref[0, r * tk:(r + 1) * tk, :]
                s_new[c, kb] = s
                bm = s[0:8]
                for i in range(8, tk, 8):
                    bm = jnp.maximum(bm, s[i:i + 8])
                m_new[c] = bm if first else jnp.maximum(m_new[c], bm)

        for r in range(nkb):
            key_block(r)

        outs = [acc_ref[c, :V_HEAD_DIM, :] / acc_ref[c, V_HEAD_DIM:V_HEAD_DIM + 1, :] for c in range(2)]
        a = (outs[0] - lam * outs[1]).T
        ms = jnp.mean(a * a, axis=-1, keepdims=True)
        y = a * lax.rsqrt(ms + EPS) * g_ref[...] * (1.0 - lam_init)
        o_ref[0] = y.astype(BF16)

    @pl.when(lax.rem(j, 2) == 0)
    def _():
        step(sa_ref, ma_ref, sb_ref, mb_ref)

    @pl.when(lax.rem(j, 2) == 1)
    def _():
        step(sb_ref, mb_ref, sa_ref, ma_ref)


def _attention(proj3, lam, subln_g, lam_init):
    b, s, _ = proj3.shape
    nh = N_HEADS
    t = ATTN_TILE
    nq = s // t
    n_items = b * nq
    augq_np, augk_np, dg_np = _alibi_tables()
    kern = functools.partial(_attn_kernel, lam_init=lam_init, n_items=n_items)
    smem = pl.BlockSpec(memory_space=pltpu.SMEM)
    cur = lambda j: jnp.minimum(j, n_items - 1)
    prev = lambda j: jnp.maximum(j - 1, 0)
    nkb = s // ATTN_KEYS
    score_scratch = pltpu.VMEM((2, nkb, ATTN_KEYS, t), F32)
    max_scratch = pltpu.VMEM((2, 8, t), F32)
    return pl.pallas_call(
        kern,
        grid=(nh, n_items + 1),
        in_specs=[smem,
                  pl.BlockSpec((1, t, V_HEAD_DIM), lambda hi, j: (cur(j) // nq, cur(j) % nq, hi)),
                  pl.BlockSpec((1, s, V_HEAD_DIM), lambda hi, j: (cur(j) // nq, 0, nh + hi)),
                  pl.BlockSpec((1, s, V_HEAD_DIM), lambda hi, j: (prev(j) // nq, 0, 2 * nh + hi)),
                  pl.BlockSpec((1, t, V_HEAD_DIM), lambda hi, j: (hi, cur(j) % nq, 0)),
                  pl.BlockSpec((1, s, V_HEAD_DIM), lambda hi, j: (hi, 0, 0)),
                  pl.BlockSpec((1, t, t), lambda hi, j: (hi, 0, 0), pipeline_mode=pl.Buffered(1)),
                  pl.BlockSpec((1, V_HEAD_DIM), lambda hi, j: (0, 0))],
        out_specs=pl.BlockSpec((1, t, V_HEAD_DIM), lambda hi, j: (prev(j) // nq, prev(j) % nq, hi)),
        out_shape=jax.ShapeDtypeStruct((b, s, ATTN_WIDTH), BF16),
        scratch_shapes=[pltpu.VMEM((2, s, V_HEAD_DIM), BF16), score_scratch, score_scratch,
                        max_scratch, max_scratch, pltpu.VMEM((nkb, PV_ROWS, ATTN_KEYS), BF16),
                        pltpu.VMEM((2, 2, t, V_HEAD_DIM), BF16), pltpu.VMEM((2, PV_ROWS, t), F32)],
        compiler_params=_params(("arbitrary", "arbitrary"), 56),
        name="diff_attn",
    )(lam, proj3, proj3, proj3, jnp.asarray(augq_np), jnp.asarray(augk_np), jnp.asarray(dg_np), subln_g)


def _gate_kernel(u1_ref, u2_ref, u3_ref, w1_ref, w2_ref, w3_ref, b1_ref, b2_ref, b3_ref, vg_ref, x1_ref):
    s_len = u1_ref.shape[1]
    row = lax.broadcasted_iota(jnp.int32, (s_len, 1), 0)

    def conv(u_ref, w_ref, b_ref):
        u = u_ref[0].astype(F32)
        prev = jnp.where(row == 0, 0.0, pltpu.roll(u, 1, 0))
        nxt = jnp.where(row == s_len - 1, 0.0, pltpu.roll(u, s_len - 1, 0))
        w = w_ref[...]
        return b_ref[...] + prev * w[0:1] + u * w[1:2] + nxt * w[2:3]

    x1 = conv(u1_ref, w1_ref, b1_ref)
    x2 = conv(u2_ref, w2_ref, b2_ref)
    v = conv(u3_ref, w3_ref, b3_ref)
    vg_ref[0] = (v * x2).astype(BF16)
    x1_ref[0] = x1.astype(BF16)


def _gate(proj3, conv_w, conv_b, tc=256):
    b, s, _ = proj3.shape
    c = HYENA_WIDTH
    nb = c // tc
    base = 3 * ATTN_WIDTH // tc

    def uspec(part):
        return pl.BlockSpec((1, s, tc), lambda bi, ci: (bi, 0, base + part * nb + ci))

    def wspec(part):
        return pl.BlockSpec((SHORT_CONV, tc), lambda bi, ci: (0, part * nb + ci))

    def bspec(part):
        return pl.BlockSpec((1, tc), lambda bi, ci: (0, part * nb + ci))

    ospec = pl.BlockSpec((1, s, tc), lambda bi, ci: (bi, 0, ci))
    return pl.pallas_call(
        _gate_kernel,
        grid=(b, nb),
        in_specs=[uspec(0), uspec(1), uspec(2), wspec(0), wspec(1), wspec(2), bspec(0), bspec(1), bspec(2)],
        out_specs=[ospec, ospec],
        out_shape=[jax.ShapeDtypeStruct((b, s, c), BF16), jax.ShapeDtypeStruct((b, s, c), BF16)],
        compiler_params=_params(("parallel", "parallel"), 48),
        name="hyena_gate",
    )(proj3, proj3, proj3, conv_w, conv_w, conv_w, conv_b, conv_b, conv_b)


def _filter_kernel(z_ref, w1_ref, b1_ref, fr_ref, w2_ref, b2_ref, w3_ref, dl_ref, o_ref, *, tr):
    hi = lax.Precision.HIGHEST
    fr = fr_ref[...]
    h = jnp.sin(fr * (jnp.dot(z_ref[...], w1_ref[...], precision=hi, preferred_element_type=F32) + b1_ref[...]))
    h = jnp.sin(fr * (jnp.dot(h, w2_ref[...], precision=hi, preferred_element_type=F32) + b2_ref[...]))
    h = jnp.dot(h.astype(BF16), w3_ref[...].astype(BF16), preferred_element_type=F32)
    row = pl.program_id(0) * tr + lax.broadcasted_iota(jnp.int32, (tr, 1), 0)
    t = row.astype(F32) * (1.0 / (SEQ - 1))
    decay = jnp.exp(-t * dl_ref[...]) + DECAY_SHIFT
    c = HYENA_WIDTH
    o_ref[:, :c] = (h[:, :c] * decay).astype(BF16)
    o_ref[:, c:] = jnp.where(row == 0, 0.0, h[:, c:] * decay).astype(BF16)


def _filters(zpad, w1pad, b1, freq, w2, b2, w3, deltas, tr=256):
    L = SEQ
    c2 = 2 * HYENA_WIDTH
    full = lambda shape: pl.BlockSpec(shape, lambda i: (0, 0))
    return pl.pallas_call(
        functools.partial(_filter_kernel, tr=tr),
        grid=(L // tr,),
        in_specs=[pl.BlockSpec((tr, LANE), lambda i: (i, 0)),
                  full((LANE, FILTER_HIDDEN)), full((1, FILTER_HIDDEN)), full((1, FILTER_HIDDEN)),
                  full((FILTER_HIDDEN, FILTER_HIDDEN)), full((1, FILTER_HIDDEN)),
                  full((FILTER_HIDDEN, c2)), full((1, HYENA_WIDTH))],
        out_specs=pl.BlockSpec((tr, c2), lambda i: (i, 0)),
        out_shape=jax.ShapeDtypeStruct((L, c2), BF16),
        compiler_params=_params(("parallel",), 48),
        name="hyena_filters",
    )(zpad, w1pad, b1, freq, w2, b2, w3, deltas)


def _kspec_kernel(f_ref, h_ref, o_ref, fb_ref, g_ref):
    c = HYENA_WIDTH
    t = DFT_TILE
    L = SEQ
    fb = f_ref[0].astype(BF16)
    fb_ref[0] = fb
    p = jnp.dot(fb, h_ref[...], preferred_element_type=F32)
    row = pl.program_id(0) * t + lax.broadcasted_iota(jnp.int32, (t, 1), 0)
    o_ref[0] = p[:t, :c] + p[:t, c:]
    o_ref[1] = jnp.where(row == 0, p[t:, :c] + p[t:, c:], p[t:, :c] - p[t:, c:])
    col = lax.broadcasted_iota(jnp.int32, (1, L), 1)
    sign = (1 - 2 * (row & 1)).astype(F32)
    gc = f_ref[0, :t, :] * jnp.where(col == 0, 1.0 / FFT_N, 2.0 / FFT_N)
    gs = jnp.where(col == 0, sign * (1.0 / FFT_N),
                   jnp.where(row == 0, 0.0, f_ref[0, t:, :] * (2.0 / FFT_N)))
    g_ref[:, :L] = gc.astype(BF16)
    g_ref[:, L:] = gs.astype(BF16)


def _kspec(table, hcat):
    L = SEQ
    c = HYENA_WIDTH
    t = DFT_TILE
    return pl.pallas_call(
        _kspec_kernel,
        grid=(L // t,),
        in_specs=[pl.BlockSpec((1, 2 * t, L), lambda i: (i, 0, 0)),
                  pl.BlockSpec((L, 2 * c), lambda i: (0, 0), pipeline_mode=pl.Buffered(1))],
        out_specs=[pl.BlockSpec((2, t, c), lambda i: (0, i, 0)),
                   pl.BlockSpec((1, 2 * t, L), lambda i: (i, 0, 0)),
                   pl.BlockSpec((t, 2 * L), lambda i: (i, 0))],
        out_shape=[jax.ShapeDtypeStruct((2, L, c), F32),
                   jax.ShapeDtypeStruct(table.shape, BF16),
                   jax.ShapeDtypeStruct((L, 2 * L), BF16)],
        compiler_params=_params(("arbitrary",), 56),
        name="hyena_kspec",
    )(table, hcat)


def _fwd_kernel(f_ref, vg_ref, k_ref, y_ref):
    t = DFT_TILE
    for bb in range(vg_ref.shape[0]):
        vg = vg_ref[bb]
        for r0 in range(0, t, ROW_CHUNK):
            xr = jnp.dot(f_ref[0, pl.ds(r0, ROW_CHUNK), :], vg, preferred_element_type=F32)
            xi = jnp.dot(f_ref[0, pl.ds(t + r0, ROW_CHUNK), :], vg, preferred_element_type=F32)
            kr = k_ref[0, pl.ds(r0, ROW_CHUNK), :]
            ki = k_ref[1, pl.ds(r0, ROW_CHUNK), :]
            row = pl.program_id(0) * t + r0 + lax.broadcasted_iota(jnp.int32, (ROW_CHUNK, 1), 0)
            first = row == 0
            yr = jnp.where(first, xr * kr, xr * kr - xi * ki)
            yi = jnp.where(first, xi * ki, xr * ki + xi * kr)
            y_ref[bb, 0, pl.ds(r0, ROW_CHUNK), :] = yr.astype(BF16)
            y_ref[bb, 1, pl.ds(r0, ROW_CHUNK), :] = yi.astype(BF16)


def _fwd_dft(table_bf16, vg, kspec, nb=2):
    b, L, c = vg.shape
    t = DFT_TILE
    return pl.pallas_call(
        _fwd_kernel,
        grid=(L // t, b // nb),
        in_specs=[pl.BlockSpec((1, 2 * t, L), lambda j, bi: (j, 0, 0)),
                  pl.BlockSpec((nb, L, c), lambda j, bi: (bi, 0, 0)),
                  pl.BlockSpec((2, t, c), lambda j, bi: (0, j, 0))],
        out_specs=pl.BlockSpec((nb, 2, t, c), lambda j, bi: (bi, 0, j, 0)),
        out_shape=jax.ShapeDtypeStruct((b, 2, L, c), BF16),
        compiler_params=_params(("arbitrary", "arbitrary"), 56),
        name="hyena_fwd_dft",
    )(table_bf16, vg, kspec)


def _inv_kernel(g_ref, y_ref, vg_ref, x1_ref, d_ref, gain_ref, o_ref):
    for r0 in range(0, o_ref.shape[1], ROW_CHUNK):
        rows = pl.ds(r0, ROW_CHUNK)
        y = jnp.dot(g_ref[rows, :], y_ref[0], preferred_element_type=F32)
        vg = vg_ref[0, rows, :].astype(F32)
        y = (y + vg * d_ref[...]) * x1_ref[0, rows, :].astype(F32)
        ms = jnp.mean(y * y, axis=-1, keepdims=True)
        o_ref[0, rows, :] = (y * lax.rsqrt(ms + EPS) * gain_ref[...]).astype(BF16)


def _inv_dft(inv_weights, yspec, vg, x1, d_skip, gain, tt=1024):
    b, L, c = vg.shape
    return pl.pallas_call(
        _inv_kernel,
        grid=(L // tt, b),
        in_specs=[pl.BlockSpec((tt, 2 * L), lambda j, bi: (j, 0)),
                  pl.BlockSpec((1, 2 * L, c), lambda j, bi: (bi, 0, 0)),
                  pl.BlockSpec((1, tt, c), lambda j, bi: (bi, j, 0)),
                  pl.BlockSpec((1, tt, c), lambda j, bi: (bi, j, 0)),
                  pl.BlockSpec((1, c), lambda j, bi: (0, 0)),
                  pl.BlockSpec((1, c), lambda j, bi: (0, 0))],
        out_specs=pl.BlockSpec((1, tt, c), lambda j, bi: (bi, j, 0)),
        out_shape=jax.ShapeDtypeStruct((b, L, c), BF16),
        compiler_params=_params(("arbitrary", "arbitrary"), 56),
        name="hyena_inv_dft",
    )(inv_weights, yspec, vg, x1, d_skip, gain)


def _layer_norm(y, g, b):
    mu = jnp.mean(y, axis=-1, keepdims=True)
    yc = y - mu
    var = jnp.mean(yc * yc, axis=-1, keepdims=True)
    return yc * lax.rsqrt(var + EPS) * g + b


def _outproj_kernel(att_ref, hy_ref, w_ref, x_ref, g_ref, b_ref, o_ref):
    a = ATTN_WIDTH
    for r0 in range(0, o_ref.shape[0], ROW_CHUNK):
        rows = pl.ds(r0, ROW_CHUNK)
        mix = jnp.dot(att_ref[rows, :], w_ref[:a, :], preferred_element_type=F32)
        mix = mix + jnp.dot(hy_ref[rows, :], w_ref[a:, :], preferred_element_type=F32)
        y = ALPHA * x_ref[rows, :] + mix
        o_ref[rows, :] = _layer_norm(y, g_ref[...], b_ref[...])


def _outproj(att2d, hy2d, w_bf16, x2d, g, b, tm=1024):
    m, d = x2d.shape
    return pl.pallas_call(
        _outproj_kernel,
        grid=(m // tm,),
        in_specs=[pl.BlockSpec((tm, ATTN_WIDTH), lambda i: (i, 0)),
                  pl.BlockSpec((tm, HYENA_WIDTH), lambda i: (i, 0)),
                  pl.BlockSpec((d, d), lambda i: (0, 0), pipeline_mode=pl.Buffered(1)),
                  pl.BlockSpec((tm, d), lambda i: (i, 0)),
                  pl.BlockSpec((1, d), lambda i: (0, 0)),
                  pl.BlockSpec((1, d), lambda i: (0, 0))],
        out_specs=pl.BlockSpec((tm, d), lambda i: (i, 0)),
        out_shape=jax.ShapeDtypeStruct((m, d), F32),
        compiler_params=_params(("parallel",), 56),
        name="outproj_ln",
    )(att2d, hy2d, w_bf16, x2d, g, b)


def _ffn_kernel(x_ref, w1_ref, w2_ref, g_ref, b_ref, o_ref, xb_ref):
    f = pl.program_id(1)
    last = pl.num_programs(1) - 1
    tm = o_ref.shape[0]

    def mlp(xb, base):
        h = jnp.dot(xb, w1_ref[...], preferred_element_type=F32)
        h = jnp.square(jnp.maximum(h, 0.0)).astype(BF16)
        return base + jnp.dot(h, w2_ref[...], preferred_element_type=F32)

    @pl.when(f == 0)
    def _():
        for r0 in range(0, tm, ROW_CHUNK):
            rows = pl.ds(r0, ROW_CHUNK)
            x = x_ref[rows, :]
            xb = x.astype(BF16)
            xb_ref[rows, :] = xb
            o_ref[rows, :] = mlp(xb, ALPHA * x)

    @pl.when(jnp.logical_and(f > 0, f < last))
    def _():
        for r0 in range(0, tm, 2 * ROW_CHUNK):
            rows = pl.ds(r0, 2 * ROW_CHUNK)
            o_ref[rows, :] = mlp(xb_ref[rows, :], o_ref[rows, :])

    @pl.when(f == last)
    def _():
        for r0 in range(0, tm, ROW_CHUNK):
            rows = pl.ds(r0, ROW_CHUNK)
            y = mlp(xb_ref[rows, :], o_ref[rows, :])
            o_ref[rows, :] = _layer_norm(y, g_ref[...], b_ref[...])


def _ffn(x2d, w1_bf16, w2_bf16, g, b, tm=1024, tf=FFN_TILE):
    m, d = x2d.shape
    dff = w1_bf16.shape[1]
    return pl.pallas_call(
        _ffn_kernel,
        grid=(m // tm, dff // tf),
        in_specs=[pl.BlockSpec((tm, d), lambda i, j: (i, 0)),
                  pl.BlockSpec((d, tf), lambda i, j: (0, j)),
                  pl.BlockSpec((tf, d), lambda i, j: (j, 0)),
                  pl.BlockSpec((1, d), lambda i, j: (0, 0)),
                  pl.BlockSpec((1, d), lambda i, j: (0, 0))],
        out_specs=pl.BlockSpec((tm, d), lambda i, j: (i, 0)),
        out_shape=jax.ShapeDtypeStruct((m, d), F32),
        scratch_shapes=[pltpu.VMEM((tm, d), BF16)],
        compiler_params=_params(("parallel", "arbitrary"), 58),
        name="ffn_ln",
    )(x2d, w1_bf16, w2_bf16, g, b)


def kernel(x, w_in, lambda_q1, lambda_k1, lambda_q2, lambda_k2, subln_g, conv_w, conv_b, filt_w1, filt_b1,
           filt_freq, filt_w2, filt_b2, filt_w3, hyena_skip, hyena_gain, w_out, ln1_g, ln1_b, w_ff1, w_ff2,
           ln2_g, ln2_b):
    B, S, D = x.shape
    assert (B, S, D) == (BATCH, SEQ, D_MODEL)
    zpad_np, deltas_np = _filter_tables()
    dft_tab = jnp.asarray(_dft_table())
    zpad = jnp.asarray(zpad_np)
    deltas = jnp.asarray(deltas_np)
    row = lambda v: v.astype(F32).reshape(1, -1)

    x2d = x.reshape(B * S, D)
    for l in range(DEPTH):
        lam_init = 0.8 - 0.6 * math.exp(-0.3 * l)
        lam = (jnp.exp(jnp.sum(lambda_q1[l].astype(F32) * lambda_k1[l].astype(F32)))
               - jnp.exp(jnp.sum(lambda_q2[l].astype(F32) * lambda_k2[l].astype(F32)))
               + lam_init).reshape(1)

        col_scale = np.ones((1, IN_COLS), np.float32)
        col_scale[:, :ATTN_WIDTH] = Q_PRESCALE
        proj, w_out_b, w_ff1_b, w_ff2_b = _inproj(
            x2d, w_in[l].astype(BF16), jnp.asarray(col_scale),
            [w_out[l].astype(F32), w_ff1[l].astype(F32), w_ff2[l].astype(F32)])
        proj3 = proj.reshape(B, S, IN_COLS)

        att = _attention(proj3, lam, row(subln_g[l]), lam_init)

        vg, x1 = _gate(proj3, conv_w[l].astype(F32), row(conv_b[l]))
        w1pad = jnp.zeros((LANE, FILTER_HIDDEN), F32).at[:FILTER_EMB].set(filt_w1[l].astype(F32))
        hcat = _filters(zpad, w1pad, row(filt_b1[l]), row(filt_freq[l]), filt_w2[l].astype(F32),
                        row(filt_b2[l]), filt_w3[l].astype(F32), deltas)
        kspec, dft_tab_b, inv_weights = _kspec(dft_tab, hcat)
        yspec = _fwd_dft(dft_tab_b, vg, kspec)
        hy = _inv_dft(inv_weights, yspec.reshape(B, 2 * S, HYENA_WIDTH), vg, x1, row(hyena_skip[l]),
                      row(hyena_gain[l]))

        x2d = _outproj(att.reshape(B * S, ATTN_WIDTH), hy.reshape(B * S, HYENA_WIDTH), w_out_b,
                       x2d, row(ln1_g[l]), row(ln1_b[l]))
        x2d = _ffn(x2d, w_ff1_b, w_ff2_b, row(ln2_g[l]), row(ln2_b[l]))
    return x2d.reshape(B, S, D)
```

```python
import functools
import math

import numpy as np
import jax
import jax.numpy as jnp
from jax import lax
from jax.experimental import pallas as pl
from jax.experimental.pallas import tpu as pltpu

D_MODEL = 2048
BATCH = 8
SEQ = 2048
DEPTH = 1
ATTN_WIDTH = D_MODEL // 2
HYENA_WIDTH = D_MODEL - ATTN_WIDTH
N_HEADS = 8
HEAD_DIM = ATTN_WIDTH // N_HEADS // 2
V_HEAD_DIM = 2 * HEAD_DIM
SHORT_CONV = 3
FILTER_EMB = 33
FILTER_HIDDEN = 64
DECAY_FAST = 0.3
DECAY_SLOW = 1.5
DECAY_TARGET = 1e-2
DECAY_SHIFT = 0.0
D_FF = 4 * D_MODEL
ALPHA = (2.0 * DEPTH) ** 0.25
EPS = 1e-5
IN_COLS = 3 * ATTN_WIDTH + 3 * HYENA_WIDTH
FFT_N = 2 * SEQ
LANE = 128
MXU_DIM = 256
DFT_TILE = 512
FFN_TILE = 1024
ROW_CHUNK = 256
ATTN_TILE = 512
ATTN_KEYS = 512
PV_ROWS = V_HEAD_DIM + 16
LOG2E = math.log2(math.e)
Q_PRESCALE = HEAD_DIM ** -0.5 * LOG2E

BF16 = jnp.bfloat16
F32 = jnp.float32

_MIB = 1024 * 1024


def _params(semantics, vmem_mib):
    return pltpu.CompilerParams(dimension_semantics=semantics, vmem_limit_bytes=vmem_mib * _MIB)


@functools.lru_cache(maxsize=None)
def _dft_table():
    k = np.arange(SEQ, dtype=np.int64)
    phase = (k[:, None] * k[None, :]) % FFT_N
    ang = (2.0 * np.pi / FFT_N) * phase
    c = np.cos(ang)
    s = -np.sin(ang)
    s[0, :] = np.where(k % 2 == 0, 1.0, -1.0)
    nt = SEQ // DFT_TILE
    tiled = np.concatenate([c.reshape(nt, DFT_TILE, SEQ), s.reshape(nt, DFT_TILE, SEQ)], axis=1)
    return tiled.astype(np.float32)


@functools.lru_cache(maxsize=None)
def _filter_tables():
    L = SEQ
    t = np.linspace(0.0, 1.0, L)[:, None]
    bands = (FILTER_EMB - 1) // 2
    w = 2.0 * np.pi * np.arange(L, dtype=np.float64)[:, None] / L
    f = np.linspace(1e-4, bands - 1, bands)[None, :]
    z = np.concatenate([t, np.cos(f * w), -np.sin(f * w)], axis=-1)
    zpad = np.zeros((L, LANE), np.float32)
    zpad[:, :FILTER_EMB] = z
    max_decay = math.log(DECAY_TARGET) / DECAY_FAST
    min_decay = math.log(DECAY_TARGET) / DECAY_SLOW
    deltas = np.abs(np.linspace(min_decay, max_decay, HYENA_WIDTH))[None, :].astype(np.float32)
    return zpad, deltas


def _inproj_kernel(x_ref, w_ref, cs_ref, wa_ref, wb_ref, wc_ref, o_ref, wa_out, wb_out, wc_out, xb_ref):
    j = pl.program_id(1)
    wa_out[...] = wa_ref[...].astype(BF16)
    wb_out[...] = wb_ref[...].astype(BF16)
    wc_out[...] = wc_ref[...].astype(BF16)

    @pl.when(j == 0)
    def _():
        for r0 in range(0, o_ref.shape[0], ROW_CHUNK):
            rows = pl.ds(r0, ROW_CHUNK)
            xb = x_ref[rows, :].astype(BF16)
            xb_ref[rows, :] = xb
            acc = jnp.dot(xb, w_ref[...], preferred_element_type=F32)
            o_ref[rows, :] = (acc * cs_ref[...]).astype(BF16)

    @pl.when(j > 0)
    def _():
        acc = jnp.dot(xb_ref[...], w_ref[...], preferred_element_type=F32)
        o_ref[...] = (acc * cs_ref[...]).astype(BF16)


def _inproj(x2d, w_bf16, col_scale, side_weights, tm=1024, tn=1536):
    m, k = x2d.shape
    n = w_bf16.shape[1]
    nj = n // tn
    steps = (m // tm) * nj

    def slab(w):
        return pl.BlockSpec((w.shape[0] // steps, w.shape[1]), lambda i, j: (i * nj + j, 0))

    side_specs = [slab(w) for w in side_weights]
    return pl.pallas_call(
        _inproj_kernel,
        grid=(m // tm, nj),
        in_specs=[pl.BlockSpec((tm, k), lambda i, j: (i, 0)),
                  pl.BlockSpec((k, tn), lambda i, j: (0, j)),
                  pl.BlockSpec((1, tn), lambda i, j: (0, j))] + side_specs,
        out_specs=[pl.BlockSpec((tm, tn), lambda i, j: (i, j))] + side_specs,
        out_shape=[jax.ShapeDtypeStruct((m, n), BF16)] + [jax.ShapeDtypeStruct(w.shape, BF16) for w in side_weights],
        scratch_shapes=[pltpu.VMEM((tm, k), BF16)],
        compiler_params=_params(("arbitrary", "arbitrary"), 56),
        name="inproj",
    )(x2d, w_bf16, col_scale, *side_weights)


@functools.lru_cache(maxsize=None)
def _alibi_tables():
    pos = np.arange(SEQ)
    hi = ((pos >> 8) << 8).astype(np.float64)
    lo = (pos & 255).astype(np.float64)
    augq = np.zeros((N_HEADS, SEQ, V_HEAD_DIM), np.float64)
    augk = np.zeros((N_HEADS, SEQ, V_HEAD_DIM), np.float64)
    t = ATTN_TILE
    ahead = np.maximum(np.arange(t)[:, None] - np.arange(t)[None, :], 0).astype(np.float64)
    dg = np.zeros((N_HEADS, t, t), np.float64)
    for h in range(N_HEADS):
        c = 2.0 ** (-8.0 * (h + 1) / N_HEADS) * LOG2E
        rest = c
        for p in range(3):
            piece = float(np.float32(rest).astype(BF16).astype(np.float64))
            rest -= piece
            for base in (0, HEAD_DIM):
                augq[h, :, base + 2 * p] = piece
                augq[h, :, base + 2 * p + 1] = piece
                augk[h, :, base + 2 * p] = hi
                augk[h, :, base + 2 * p + 1] = lo
                augq[h, :, base + 6 + 2 * p] = -hi
                augq[h, :, base + 6 + 2 * p + 1] = -lo
                augk[h, :, base + 6 + 2 * p] = piece
                augk[h, :, base + 6 + 2 * p + 1] = piece
        dg[h] = -2.0 * c * ahead
    return augq.astype(BF16), augk.astype(BF16), dg.astype(np.float32)


def _attn_kernel(lam_ref, q_ref, k_ref, v_ref, aq_ref, ak_ref, dg_ref, g_ref, o_ref,
                 kt_ref, sa_ref, sb_ref, ma_ref, mb_ref, vt_ref, ql_ref, acc_ref, *, lam_init, n_items):
    t = ATTN_TILE
    tk = ATTN_KEYS
    nkb = SEQ // tk
    nq = SEQ // t
    diag_blocks = t // tk
    j = pl.program_id(0)
    qb = lax.rem(jnp.minimum(j, n_items - 1), nq)
    lam = lam_ref[0]
    lane = lax.broadcasted_iota(jnp.int32, (1, V_HEAD_DIM), 1)
    first_half = lane < HEAD_DIM
    nt_dims = (((1,), (1,)), ((), ()))

    @pl.when(j == 0)
    def _():
        sb_ref[...] = jnp.zeros_like(sb_ref)
        mb_ref[...] = jnp.zeros_like(mb_ref)

    @pl.when(qb == 0)
    def _():
        k = k_ref[0]
        ak = ak_ref[0]
        kt_ref[0] = jnp.where(first_half, k, ak)
        kt_ref[1] = jnp.where(first_half, ak, k)

    @pl.when(lax.rem(jnp.maximum(j - 1, 0), nq) == 0)
    def _():
        pad_row = lax.broadcasted_iota(jnp.int32, (PV_ROWS - V_HEAD_DIM, 1), 0)
        ones_row = jnp.where(pad_row == 0, 1.0, 0.0).astype(BF16)
        for kb in range(nkb):
            vt_ref[kb, :V_HEAD_DIM, :] = v_ref[0, kb * tk:(kb + 1) * tk, :].astype(F32).T.astype(BF16)
            vt_ref[kb, V_HEAD_DIM:, :] = jnp.broadcast_to(ones_row, (PV_ROWS - V_HEAD_DIM, tk))

    def step(s_new, m_new, s_old, m_old):
        q = q_ref[0]
        aq = aq_ref[0]
        naq = -aq
        for c, keep in enumerate((first_half, jnp.logical_not(first_half))):
            ql_ref[c, 0] = jnp.where(keep, q, aq)
            ql_ref[c, 1] = jnp.where(keep, q, naq)
        col_max = [jnp.max(m_old[c], axis=0, keepdims=True) for c in range(2)]

        def key_block(r):
            first = r == 0
            shared = r < diag_blocks
            start = qb * diag_blocks + r
            wrapped = start >= nkb
            kb = jnp.where(wrapped, start - nkb, start)
            side = 0 if shared else jnp.where(wrapped, 0, 1)
            for c in range(2):
                p = jnp.exp2(s_old[c, r] - col_max[c]).astype(BF16)
                part = jnp.dot(vt_ref[r], p, preferred_element_type=F32)
                acc_ref[c] = part if first else acc_ref[c] + part
                s = lax.dot_general(kt_ref[c, pl.ds(pl.multiple_of(kb * tk, tk), tk), :], ql_ref[c, side], nt_dims,
                                    preferred_element_type=F32)
                if shared:
                    s = s + dg_ref[0, r * tk:(r + 1) * tk, :]
                s_new[c, kb] = s
                bm = s[0:8]
                for i in range(8, tk, 8):
                    bm = jnp.maximum(bm, s[i:i + 8])
                m_new[c] = bm if first else jnp.maximum(m_new[c], bm)

        for r in range(nkb):
            key_block(r)

        outs = [acc_ref[c, :V_HEAD_DIM, :] / acc_ref[c, V_HEAD_DIM:V_HEAD_DIM + 1, :] for c in range(2)]
        a = (outs[0] - lam * outs[1]).T
        ms = jnp.mean(a * a, axis=-1, keepdims=True)
        y = a * lax.rsqrt(ms + EPS) * g_ref[...] * (1.0 - lam_init)
        o_ref[0] = y.astype(BF16)

    @pl.when(lax.rem(j, 2) == 0)
    def _():
        step(sa_ref, ma_ref, sb_ref, mb_ref)

    @pl.when(lax.rem(j, 2) == 1)
    def _():
        step(sb_ref, mb_ref, sa_ref, ma_ref)


def _attention(proj3, lam, subln_g, lam_init):
    b, s, _ = proj3.shape
    nh = N_HEADS
    t = ATTN_TILE
    nq = s // t
    per_head = b * nq
    n_items = nh * per_head
    augq_np, augk_np, dg_np = _alibi_tables()
    kern = functools.partial(_attn_kernel, lam_init=lam_init, n_items=n_items)
    smem = pl.BlockSpec(memory_space=pltpu.SMEM)
    cur = lambda j: jnp.minimum(j, n_items - 1)
    prev = lambda j: jnp.maximum(j - 1, 0)
    head = lambda i: i // per_head
    batch = lambda i: (i % per_head) // nq
    qblk = lambda i: i % nq
    nkb = s // ATTN_KEYS
    score_scratch = pltpu.VMEM((2, nkb, ATTN_KEYS, t), F32)
    max_scratch = pltpu.VMEM((2, 8, t), F32)
    return pl.pallas_call(
        kern,
        grid=(n_items + 1,),
        in_specs=[smem,
                  pl.BlockSpec((1, t, V_HEAD_DIM), lambda j: (batch(cur(j)), qblk(cur(j)), head(cur(j)))),
                  pl.BlockSpec((1, s, V_HEAD_DIM), lambda j: (batch(cur(j)), 0, nh + head(cur(j)))),
                  pl.BlockSpec((1, s, V_HEAD_DIM), lambda j: (batch(prev(j)), 0, 2 * nh + head(prev(j)))),
                  pl.BlockSpec((1, t, V_HEAD_DIM), lambda j: (head(cur(j)), qblk(cur(j)), 0)),
                  pl.BlockSpec((1, s, V_HEAD_DIM), lambda j: (head(cur(j)), 0, 0)),
                  pl.BlockSpec((1, t, t), lambda j: (head(cur(j)), 0, 0), pipeline_mode=pl.Buffered(1)),
                  pl.BlockSpec((1, V_HEAD_DIM), lambda j: (0, 0))],
        out_specs=pl.BlockSpec((1, t, V_HEAD_DIM), lambda j: (batch(prev(j)), qblk(prev(j)), head(prev(j)))),
        out_shape=jax.ShapeDtypeStruct((b, s, ATTN_WIDTH), BF16),
        scratch_shapes=[pltpu.VMEM((2, s, V_HEAD_DIM), BF16), score_scratch, score_scratch,
                        max_scratch, max_scratch, pltpu.VMEM((nkb, PV_ROWS, ATTN_KEYS), BF16),
                        pltpu.VMEM((2, 2, t, V_HEAD_DIM), BF16), pltpu.VMEM((2, PV_ROWS, t), F32)],
        compiler_params=_params(("arbitrary",), 56),
        name="diff_attn",
    )(lam, proj3, proj3, proj3, jnp.asarray(augq_np), jnp.asarray(augk_np), jnp.asarray(dg_np), subln_g)


def _gate_kernel(u1_ref, u2_ref, u3_ref, w1_ref, w2_ref, w3_ref, b1_ref, b2_ref, b3_ref, vg_ref, x1_ref):
    s_len = u1_ref.shape[1]
    row = lax.broadcasted_iota(jnp.int32, (s_len, 1), 0)

    def conv(u_ref, w_ref, b_ref):
        u = u_ref[0].astype(F32)
        prev = jnp.where(row == 0, 0.0, pltpu.roll(u, 1, 0))
        nxt = jnp.where(row == s_len - 1, 0.0, pltpu.roll(u, s_len - 1, 0))
        w = w_ref[...]
        return b_ref[...] + prev * w[0:1] + u * w[1:2] + nxt * w[2:3]

    x1 = conv(u1_ref, w1_ref, b1_ref)
    x2 = conv(u2_ref, w2_ref, b2_ref)
    v = conv(u3_ref, w3_ref, b3_ref)
    vg_ref[0] = (v * x2).astype(BF16)
    x1_ref[0] = x1.astype(BF16)


def _gate(proj3, conv_w, conv_b, tc=256):
    b, s, _ = proj3.shape
    c = HYENA_WIDTH
    nb = c // tc
    base = 3 * ATTN_WIDTH // tc

    def uspec(part):
        return pl.BlockSpec((1, s, tc), lambda bi, ci: (bi, 0, base + part * nb + ci))

    def wspec(part):
        return pl.BlockSpec((SHORT_CONV, tc), lambda bi, ci: (0, part * nb + ci))

    def bspec(part):
        return pl.BlockSpec((1, tc), lambda bi, ci: (0, part * nb + ci))

    ospec = pl.BlockSpec((1, s, tc), lambda bi, ci: (bi, 0, ci))
    return pl.pallas_call(
        _gate_kernel,
        grid=(b, nb),
        in_specs=[uspec(0), uspec(1), uspec(2), wspec(0), wspec(1), wspec(2), bspec(0), bspec(1), bspec(2)],
        out_specs=[ospec, ospec],
        out_shape=[jax.ShapeDtypeStruct((b, s, c), BF16), jax.ShapeDtypeStruct((b, s, c), BF16)],
        compiler_params=_params(("parallel", "parallel"), 48),
        name="hyena_gate",
    )(proj3, proj3, proj3, conv_w, conv_w, conv_w, conv_b, conv_b, conv_b)


def _filter_kernel(z_ref, w1_ref, b1_ref, fr_ref, w2_ref, b2_ref, w3_ref, dl_ref, o_ref, *, tr):
    hi = lax.Precision.HIGHEST
    fr = fr_ref[...]
    h = jnp.sin(fr * (jnp.dot(z_ref[...], w1_ref[...], precision=hi, preferred_element_type=F32) + b1_ref[...]))
    h = jnp.sin(fr * (jnp.dot(h, w2_ref[...], precision=hi, preferred_element_type=F32) + b2_ref[...]))
    h = jnp.dot(h.astype(BF16), w3_ref[...].astype(BF16), preferred_element_type=F32)
    row = pl.program_id(0) * tr + lax.broadcasted_iota(jnp.int32, (tr, 1), 0)
    t = row.astype(F32) * (1.0 / (SEQ - 1))
    decay = jnp.exp(-t * dl_ref[...]) + DECAY_SHIFT
    c = HYENA_WIDTH
    o_ref[:, :c] = (h[:, :c] * decay).astype(BF16)
    o_ref[:, c:] = jnp.where(row == 0, 0.0, h[:, c:] * decay).astype(BF16)


def _filters(zpad, w1pad, b1, freq, w2, b2, w3, deltas, tr=256):
    L = SEQ
    c2 = 2 * HYENA_WIDTH
    full = lambda shape: pl.BlockSpec(shape, lambda i: (0, 0))
    return pl.pallas_call(
        functools.partial(_filter_kernel, tr=tr),
        grid=(L // tr,),
        in_specs=[pl.BlockSpec((tr, LANE), lambda i: (i, 0)),
                  full((LANE, FILTER_HIDDEN)), full((1, FILTER_HIDDEN)), full((1, FILTER_HIDDEN)),
                  full((FILTER_HIDDEN, FILTER_HIDDEN)), full((1, FILTER_HIDDEN)),
                  full((FILTER_HIDDEN, c2)), full((1, HYENA_WIDTH))],
        out_specs=pl.BlockSpec((tr, c2), lambda i: (i, 0)),
        out_shape=jax.ShapeDtypeStruct((L, c2), BF16),
        compiler_params=_params(("parallel",), 48),
        name="hyena_filters",
    )(zpad, w1pad, b1, freq, w2, b2, w3, deltas)


def _kspec_kernel(f_ref, h_ref, o_ref, fb_ref, g_ref):
    c = HYENA_WIDTH
    t = DFT_TILE
    L = SEQ
    fb = f_ref[0].astype(BF16)
    fb_ref[0] = fb
    p = jnp.dot(fb, h_ref[...], preferred_element_type=F32)
    row = pl.program_id(0) * t + lax.broadcasted_iota(jnp.int32, (t, 1), 0)
    o_ref[0] = p[:t, :c] + p[:t, c:]
    o_ref[1] = jnp.where(row == 0, p[t:, :c] + p[t:, c:], p[t:, :c] - p[t:, c:])
    col = lax.broadcasted_iota(jnp.int32, (1, L), 1)
    sign = (1 - 2 * (row & 1)).astype(F32)
    gc = f_ref[0, :t, :] * jnp.where(col == 0, 1.0 / FFT_N, 2.0 / FFT_N)
    gs = jnp.where(col == 0, sign * (1.0 / FFT_N),
                   jnp.where(row == 0, 0.0, f_ref[0, t:, :] * (2.0 / FFT_N)))
    g_ref[:, :L] = gc.astype(BF16)
    g_ref[:, L:] = gs.astype(BF16)


def _kspec(table, hcat):
    L = SEQ
    c = HYENA_WIDTH
    t = DFT_TILE
    return pl.pallas_call(
        _kspec_kernel,
        grid=(L // t,),
        in_specs=[pl.BlockSpec((1, 2 * t, L), lambda i: (i, 0, 0)),
                  pl.BlockSpec((L, 2 * c), lambda i: (0, 0), pipeline_mode=pl.Buffered(1))],
        out_specs=[pl.BlockSpec((2, t, c), lambda i: (0, i, 0)),
                   pl.BlockSpec((1, 2 * t, L), lambda i: (i, 0, 0)),
                   pl.BlockSpec((t, 2 * L), lambda i: (i, 0))],
        out_shape=[jax.ShapeDtypeStruct((2, L, c), F32),
                   jax.ShapeDtypeStruct(table.shape, BF16),
                   jax.ShapeDtypeStruct((L, 2 * L), BF16)],
        compiler_params=_params(("arbitrary",), 56),
        name="hyena_kspec",
    )(table, hcat)


def _fwd_kernel(f_ref, vg_ref, k_ref, y_ref):
    t = DFT_TILE
    for bb in range(vg_ref.shape[0]):
        vg = vg_ref[bb]
        for r0 in range(0, t, ROW_CHUNK):
            xr = jnp.dot(f_ref[0, pl.ds(r0, ROW_CHUNK), :], vg, preferred_element_type=F32)
            xi = jnp.dot(f_ref[0, pl.ds(t + r0, ROW_CHUNK), :], vg, preferred_element_type=F32)
            kr = k_ref[0, pl.ds(r0, ROW_CHUNK), :]
            ki = k_ref[1, pl.ds(r0, ROW_CHUNK), :]
            row = pl.program_id(0) * t + r0 + lax.broadcasted_iota(jnp.int32, (ROW_CHUNK, 1), 0)
            first = row == 0
            yr = jnp.where(first, xr * kr, xr * kr - xi * ki)
            yi = jnp.where(first, xi * ki, xr * ki + xi * kr)
            y_ref[bb, 0, pl.ds(r0, ROW_CHUNK), :] = yr.astype(BF16)
            y_ref[bb, 1, pl.ds(r0, ROW_CHUNK), :] = yi.astype(BF16)


def _fwd_dft(table_bf16, vg, kspec, nb=2):
    b, L, c = vg.shape
    t = DFT_TILE
    return pl.pallas_call(
        _fwd_kernel,
        grid=(L // t, b // nb),
        in_specs=[pl.BlockSpec((1, 2 * t, L), lambda j, bi: (j, 0, 0)),
                  pl.BlockSpec((nb, L, c), lambda j, bi: (bi, 0, 0)),
                  pl.BlockSpec((2, t, c), lambda j, bi: (0, j, 0))],
        out_specs=pl.BlockSpec((nb, 2, t, c), lambda j, bi: (bi, 0, j, 0)),
        out_shape=jax.ShapeDtypeStruct((b, 2, L, c), BF16),
        compiler_params=_params(("arbitrary", "arbitrary"), 56),
        name="hyena_fwd_dft",
    )(table_bf16, vg, kspec)


def _inv_kernel(g_ref, y_ref, vg_ref, x1_ref, d_ref, gain_ref, o_ref):
    for r0 in range(0, o_ref.shape[1], ROW_CHUNK):
        rows = pl.ds(r0, ROW_CHUNK)
        y = jnp.dot(g_ref[rows, :], y_ref[0], preferred_element_type=F32)
        vg = vg_ref[0, rows, :].astype(F32)
        y = (y + vg * d_ref[...]) * x1_ref[0, rows, :].astype(F32)
        ms = jnp.mean(y * y, axis=-1, keepdims=True)
        o_ref[0, rows, :] = (y * lax.rsqrt(ms + EPS) * gain_ref[...]).astype(BF16)


def _inv_dft(inv_weights, yspec, vg, x1, d_skip, gain, tt=1024):
    b, L, c = vg.shape
    return pl.pallas_call(
        _inv_kernel,
        grid=(L // tt, b),
        in_specs=[pl.BlockSpec((tt, 2 * L), lambda j, bi: (j, 0)),
                  pl.BlockSpec((1, 2 * L, c), lambda j, bi: (bi, 0, 0)),
                  pl.BlockSpec((1, tt, c), lambda j, bi: (bi, j, 0)),
                  pl.BlockSpec((1, tt, c), lambda j, bi: (bi, j, 0)),
                  pl.BlockSpec((1, c), lambda j, bi: (0, 0)),
                  pl.BlockSpec((1, c), lambda j, bi: (0, 0))],
        out_specs=pl.BlockSpec((1, tt, c), lambda j, bi: (bi, j, 0)),
        out_shape=jax.ShapeDtypeStruct((b, L, c), BF16),
        compiler_params=_params(("arbitrary", "arbitrary"), 56),
        name="hyena_inv_dft",
    )(inv_weights, yspec, vg, x1, d_skip, gain)


def _layer_norm(y, g, b):
    mu = jnp.mean(y, axis=-1, keepdims=True)
    yc = y - mu
    var = jnp.mean(yc * yc, axis=-1, keepdims=True)
    return yc * lax.rsqrt(var + EPS) * g + b


def _outproj_kernel(att_ref, hy_ref, w_ref, x_ref, g_ref, b_ref, o_ref):
    a = ATTN_WIDTH
    for r0 in range(0, o_ref.shape[0], ROW_CHUNK):
        rows = pl.ds(r0, ROW_CHUNK)
        mix = jnp.dot(att_ref[rows, :], w_ref[:a, :], preferred_element_type=F32)
        mix = mix + jnp.dot(hy_ref[rows, :], w_ref[a:, :], preferred_element_type=F32)
        y = ALPHA * x_ref[rows, :] + mix
        o_ref[rows, :] = _layer_norm(y, g_ref[...], b_ref[...])


def _outproj(att2d, hy2d, w_bf16, x2d, g, b, tm=1024):
    m, d = x2d.shape
    return pl.pallas_call(
        _outproj_kernel,
        grid=(m // tm,),
        in_specs=[pl.BlockSpec((tm, ATTN_WIDTH), lambda i: (i, 0)),
                  pl.BlockSpec((tm, HYENA_WIDTH), lambda i: (i, 0)),
                  pl.BlockSpec((d, d), lambda i: (0, 0), pipeline_mode=pl.Buffered(1)),
                  pl.BlockSpec((tm, d), lambda i: (i, 0)),
                  pl.BlockSpec((1, d), lambda i: (0, 0)),
                  pl.BlockSpec((1, d), lambda i: (0, 0))],
        out_specs=pl.BlockSpec((tm, d), lambda i: (i, 0)),
        out_shape=jax.ShapeDtypeStruct((m, d), F32),
        compiler_params=_params(("parallel",), 56),
        name="outproj_ln",
    )(att2d, hy2d, w_bf16, x2d, g, b)


def _ffn_kernel(x_ref, w1_ref, w2_ref, g_ref, b_ref, o_ref, xb_ref):
    f = pl.program_id(1)
    last = pl.num_programs(1) - 1
    tm = o_ref.shape[0]

    def mlp(xb, base):
        h = jnp.dot(xb, w1_ref[...], preferred_element_type=F32)
        h = jnp.square(jnp.maximum(h, 0.0)).astype(BF16)
        return base + jnp.dot(h, w2_ref[...], preferred_element_type=F32)

    @pl.when(f == 0)
    def _():
        for r0 in range(0, tm, ROW_CHUNK):
            rows = pl.ds(r0, ROW_CHUNK)
            x = x_ref[rows, :]
            xb = x.astype(BF16)
            xb_ref[rows, :] = xb
            o_ref[rows, :] = mlp(xb, ALPHA * x)

    @pl.when(jnp.logical_and(f > 0, f < last))
    def _():
        for r0 in range(0, tm, 2 * ROW_CHUNK):
            rows = pl.ds(r0, 2 * ROW_CHUNK)
            o_ref[rows, :] = mlp(xb_ref[rows, :], o_ref[rows, :])

    @pl.when(f == last)
    def _():
        for r0 in range(0, tm, ROW_CHUNK):
            rows = pl.ds(r0, ROW_CHUNK)
            y = mlp(xb_ref[rows, :], o_ref[rows, :])
            o_ref[rows, :] = _layer_norm(y, g_ref[...], b_ref[...])


def _ffn(x2d, w1_bf16, w2_bf16, g, b, tm=1024, tf=FFN_TILE):
    m, d = x2d.shape
    dff = w1_bf16.shape[1]
    return pl.pallas_call(
        _ffn_kernel,
        grid=(m // tm, dff // tf),
        in_specs=[pl.BlockSpec((tm, d), lambda i, j: (i, 0)),
                  pl.BlockSpec((d, tf), lambda i, j: (0, j)),
                  pl.BlockSpec((tf, d), lambda i, j: (j, 0)),
                  pl.BlockSpec((1, d), lambda i, j: (0, 0)),
                  pl.BlockSpec((1, d), lambda i, j: (0, 0))],
        out_specs=pl.BlockSpec((tm, d), lambda i, j: (i, 0)),
        out_shape=jax.ShapeDtypeStruct((m, d), F32),
        scratch_shapes=[pltpu.VMEM((tm, d), BF16)],
        compiler_params=_params(("parallel", "arbitrary"), 58),
        name="ffn_ln",
    )(x2d, w1_bf16, w2_bf16, g, b)


def kernel(x, w_in, lambda_q1, lambda_k1, lambda_q2, lambda_k2, subln_g, conv_w, conv_b, filt_w1, filt_b1,
           filt_freq, filt_w2, filt_b2, filt_w3, hyena_skip, hyena_gain, w_out, ln1_g, ln1_b, w_ff1, w_ff2,
           ln2_g, ln2_b):
    B, S, D = x.shape
    assert (B, S, D) == (BATCH, SEQ, D_MODEL)
    zpad_np, deltas_np = _filter_tables()
    dft_tab = jnp.asarray(_dft_table())
    zpad = jnp.asarray(zpad_np)
    deltas = jnp.asarray(deltas_np)
    row = lambda v: v.astype(F32).reshape(1, -1)

    x2d = x.reshape(B * S, D)
    for l in range(DEPTH):
        lam_init = 0.8 - 0.6 * math.exp(-0.3 * l)
        lam = (jnp.exp(jnp.sum(lambda_q1[l].astype(F32) * lambda_k1[l].astype(F32)))
               - jnp.exp(jnp.sum(lambda_q2[l].astype(F32) * lambda_k2[l].astype(F32)))
               + lam_init).reshape(1)

        col_scale = np.ones((1, IN_COLS), np.float32)
        col_scale[:, :ATTN_WIDTH] = Q_PRESCALE
        proj, w_out_b, w_ff1_b, w_ff2_b = _inproj(
            x2d, w_in[l].astype(BF16), jnp.asarray(col_scale),
            [w_out[l].astype(F32), w_ff1[l].astype(F32), w_ff2[l].astype(F32)])
        proj3 = proj.reshape(B, S, IN_COLS)

        att = _attention(proj3, lam, row(subln_g[l]), lam_init)

        vg, x1 = _gate(proj3, conv_w[l].astype(F32), row(conv_b[l]))
        w1pad = jnp.zeros((LANE, FILTER_HIDDEN), F32).at[:FILTER_EMB].set(filt_w1[l].astype(F32))
        hcat = _filters(zpad, w1pad, row(filt_b1[l]), row(filt_freq[l]), filt_w2[l].astype(F32),
                        row(filt_b2[l]), filt_w3[l].astype(F32), deltas)
        kspec, dft_tab_b, inv_weights = _kspec(dft_tab, hcat)
        yspec = _fwd_dft(dft_tab_b, vg, kspec)
        hy = _inv_dft(inv_weights, yspec.reshape(B, 2 * S, HYENA_WIDTH), vg, x1, row(hyena_skip[l]),
                      row(hyena_gain[l]))

        x2d = _outproj(att.reshape(B * S, ATTN_WIDTH), hy.reshape(B * S, HYENA_WIDTH), w_out_b,
                       x2d, row(ln1_g[l]), row(ln1_b[l]))
        x2d = _ffn(x2d, w_ff1_b, w_ff2_b, row(ln2_g[l]), row(ln2_b[l]))
    return x2d.reshape(B, S, D)
```

```python
import functools
import math

import numpy as np
import jax
import jax.numpy as jnp
from jax import lax
from jax.experimental import pallas as pl
from jax.experimental.pallas import tpu as pltpu

D_MODEL = 2048
BATCH = 8
SEQ = 2048
DEPTH = 1
ATTN_WIDTH = D_MODEL // 2
HYENA_WIDTH = D_MODEL - ATTN_WIDTH
N_HEADS = 8
HEAD_DIM = ATTN_WIDTH // N_HEADS // 2
V_HEAD_DIM = 2 * HEAD_DIM
SHORT_CONV = 3
FILTER_EMB = 33
FILTER_HIDDEN = 64
DECAY_FAST = 0.3
DECAY_SLOW = 1.5
DECAY_TARGET = 1e-2
DECAY_SHIFT = 0.0
D_FF = 4 * D_MODEL
ALPHA = (2.0 * DEPTH) ** 0.25
EPS = 1e-5
IN_COLS = 3 * ATTN_WIDTH + 3 * HYENA_WIDTH
FFT_N = 2 * SEQ
LANE = 128
MXU_DIM = 256
DFT_TILE = 512
FFN_TILE = 1024
ROW_CHUNK = 256
ATTN_TILE = 512
ATTN_KEYS = 512
HALO = 16
ATTN_COL_TILES = 2
PV_ROWS = V_HEAD_DIM + 16
LOG2E = math.log2(math.e)
Q_PRESCALE = HEAD_DIM ** -0.5 * LOG2E

BF16 = jnp.bfloat16
F32 = jnp.float32

_MIB = 1024 * 1024


def _params(semantics, vmem_mib):
    return pltpu.CompilerParams(dimension_semantics=semantics, vmem_limit_bytes=vmem_mib * _MIB)


@functools.lru_cache(maxsize=None)
def _dft_table():
    k = np.arange(SEQ, dtype=np.int64)
    phase = (k[:, None] * k[None, :]) % FFT_N
    ang = (2.0 * np.pi / FFT_N) * phase
    c = np.cos(ang)
    s = -np.sin(ang)
    s[0, :] = np.where(k % 2 == 0, 1.0, -1.0)
    nt = SEQ // DFT_TILE
    tiled = np.concatenate([c.reshape(nt, DFT_TILE, SEQ), s.reshape(nt, DFT_TILE, SEQ)], axis=1)
    return tiled.astype(np.float32)


@functools.lru_cache(maxsize=None)
def _filter_tables():
    L = SEQ
    t = np.linspace(0.0, 1.0, L)[:, None]
    bands = (FILTER_EMB - 1) // 2
    w = 2.0 * np.pi * np.arange(L, dtype=np.float64)[:, None] / L
    f = np.linspace(1e-4, bands - 1, bands)[None, :]
    z = np.concatenate([t, np.cos(f * w), -np.sin(f * w)], axis=-1)
    zpad = np.zeros((L, LANE), np.float32)
    zpad[:, :FILTER_EMB] = z
    max_decay = math.log(DECAY_TARGET) / DECAY_FAST
    min_decay = math.log(DECAY_TARGET) / DECAY_SLOW
    deltas = np.abs(np.linspace(min_decay, max_decay, HYENA_WIDTH))[None, :].astype(np.float32)
    return zpad, deltas


def _inproj_kernel(x_ref, xp_ref, xn_ref, wa_ref, wb_ref, wc_ref,
                   cwa_ref, cwb_ref, cwc_ref, cba_ref, cbb_ref, cbc_ref, ea_ref, eb_ref, ec_ref,
                   o_ref, vg_ref, x1_ref, ea_out, eb_out, ec_out, xe_ref):
    i = pl.program_id(0)
    j = pl.program_id(1)
    tm = x_ref.shape[0]
    tc = wa_ref.shape[1]
    tiles_per_seq = SEQ // tm
    w_refs = (wa_ref, wb_ref, wc_ref)
    ea_out[...] = ea_ref[...].astype(BF16)
    eb_out[...] = eb_ref[...].astype(BF16)
    ec_out[...] = ec_ref[...].astype(BF16)

    @pl.when(j == 0)
    def _():
        pos = lax.rem(i, tiles_per_seq)
        xe_ref[:HALO, :] = jnp.where(pos == 0, 0.0, xp_ref[...]).astype(BF16)
        xe_ref[HALO + tm:, :] = jnp.where(pos == tiles_per_seq - 1, 0.0, xn_ref[...]).astype(BF16)
        for r0 in range(0, tm, ROW_CHUNK):
            xb = x_ref[pl.ds(r0, ROW_CHUNK), :].astype(BF16)
            xe_ref[pl.ds(HALO + r0, ROW_CHUNK), :] = xb
            for p, w_ref in enumerate(w_refs):
                acc = jnp.dot(xb, w_ref[...], preferred_element_type=F32)
                if (p + 1) * tc <= ATTN_WIDTH:
                    acc = acc * Q_PRESCALE
                o_ref[pl.ds(r0, ROW_CHUNK), p * tc:(p + 1) * tc] = acc.astype(BF16)

    @pl.when(jnp.logical_and(j > 0, j < ATTN_COL_TILES))
    def _():
        xb = xe_ref[HALO:HALO + tm, :]
        for p, w_ref in enumerate(w_refs):
            o_ref[:, p * tc:(p + 1) * tc] = jnp.dot(xb, w_ref[...], preferred_element_type=F32).astype(BF16)

    @pl.when(j >= ATTN_COL_TILES)
    def _():
        half = tm // 2
        for c0 in range(0, tc, MXU_DIM):
            cols = slice(c0, c0 + MXU_DIM)
            for r0 in (0, half):
                xe = xe_ref[r0:r0 + half + 2 * HALO, :]
                rows = half + 2 * HALO
                z = []
                for w_ref, cw_ref, cb_ref in zip(w_refs, (cwa_ref, cwb_ref, cwc_ref), (cba_ref, cbb_ref, cbc_ref)):
                    u = jnp.dot(xe, w_ref[:, cols], preferred_element_type=F32)
                    cw = cw_ref[:, cols]
                    zz = (cb_ref[:, cols] + pltpu.roll(u, 1, 0) * cw[0:1] + u * cw[1:2]
                          + pltpu.roll(u, rows - 1, 0) * cw[2:3])
                    z.append(zz[HALO:HALO + half])
                x1, x2, v = z
                vg_ref[r0:r0 + half, cols] = (v * x2).astype(BF16)
                x1_ref[r0:r0 + half, cols] = x1.astype(BF16)


def _inproj(x2d, w_bf16, conv_w, conv_b, side_weights, tm=1024, tc=512):
    m, k = x2d.shape
    attn_cols = 3 * ATTN_WIDTH
    na = ATTN_COL_TILES
    assert attn_cols == na * 3 * tc
    nh_tiles = HYENA_WIDTH // tc
    nj = na + nh_tiles
    steps = (m // tm) * nj
    halo_per_tile = tm // HALO

    def col_block(p):
        return lambda i, j: (0, jnp.where(j < na, 3 * j + p, 3 * na + nh_tiles * p + (j - na)))

    def conv_block(p):
        return lambda i, j: (0, nh_tiles * p + jnp.maximum(j - na, 0))

    def slab(w):
        return pl.BlockSpec((w.shape[0] // steps, w.shape[1]), lambda i, j: (i * nj + j, 0))

    side_specs = [slab(w) for w in side_weights]
    hy_out = pl.BlockSpec((tm, tc), lambda i, j: (i, jnp.maximum(j - na, 0)))
    return pl.pallas_call(
        _inproj_kernel,
        grid=(m // tm, nj),
        in_specs=[pl.BlockSpec((tm, k), lambda i, j: (i, 0)),
                  pl.BlockSpec((HALO, k), lambda i, j: (jnp.maximum(i * halo_per_tile - 1, 0), 0)),
                  pl.BlockSpec((HALO, k), lambda i, j: (jnp.minimum((i + 1) * halo_per_tile, m // HALO - 1), 0))]
                 + [pl.BlockSpec((k, tc), col_block(p)) for p in range(3)]
                 + [pl.BlockSpec((SHORT_CONV, tc), conv_block(p)) for p in range(3)]
                 + [pl.BlockSpec((1, tc), conv_block(p)) for p in range(3)]
                 + side_specs,
        out_specs=[pl.BlockSpec((tm, 3 * tc), lambda i, j: (i, jnp.minimum(j, na - 1))), hy_out, hy_out] + side_specs,
        out_shape=[jax.ShapeDtypeStruct((m, attn_cols), BF16),
                   jax.ShapeDtypeStruct((m, HYENA_WIDTH), BF16),
                   jax.ShapeDtypeStruct((m, HYENA_WIDTH), BF16)]
                  + [jax.ShapeDtypeStruct(w.shape, BF16) for w in side_weights],
        scratch_shapes=[pltpu.VMEM((tm + 2 * HALO, k), BF16)],
        compiler_params=_params(("arbitrary", "arbitrary"), 58),
        name="inproj",
    )(x2d, x2d, x2d, w_bf16, w_bf16, w_bf16, conv_w, conv_w, conv_w, conv_b, conv_b, conv_b, *side_weights)


@functools.lru_cache(maxsize=None)
def _alibi_tables():
    pos = np.arange(SEQ)
    hi = ((pos >> 8) << 8).astype(np.float64)
    lo = (pos & 255).astype(np.float64)
    augq = np.zeros((N_HEADS, SEQ, V_HEAD_DIM), np.float64)
    augk = np.zeros((N_HEADS, SEQ, V_HEAD_DIM), np.float64)
    t = ATTN_TILE
    ahead = np.maximum(np.arange(t)[:, None] - np.arange(t)[None, :], 0).astype(np.float64)
    dg = np.zeros((N_HEADS, t, t), np.float64)
    for h in range(N_HEADS):
        c = 2.0 ** (-8.0 * (h + 1) / N_HEADS) * LOG2E
        rest = c
        for p in range(3):
            piece = float(np.float32(rest).astype(BF16).astype(np.float64))
            rest -= piece
            for base in (0, HEAD_DIM):
                augq[h, :, base + 2 * p] = piece
                augq[h, :, base + 2 * p + 1] = piece
                augk[h, :, base + 2 * p] = hi
                augk[h, :, base + 2 * p + 1] = lo
                augq[h, :, base + 6 + 2 * p] = -hi
                augq[h, :, base + 6 + 2 * p + 1] = -lo
                augk[h, :, base + 6 + 2 * p] = piece
                augk[h, :, base + 6 + 2 * p + 1] = piece
        dg[h] = -2.0 * c * ahead
    return augq.astype(BF16), augk.astype(BF16), dg.astype(np.float32)


def _attn_kernel(lam_ref, q_ref, k_ref, v_ref, aq_ref, ak_ref, dg_ref, g_ref, o_ref,
                 kt_ref, sa_ref, sb_ref, ma_ref, mb_ref, vt_ref, ql_ref, acc_ref, *, lam_init, n_items):
    t = ATTN_TILE
    tk = ATTN_KEYS
    nkb = SEQ // tk
    nq = SEQ // t
    diag_blocks = t // tk
    j = pl.program_id(0)
    qb = lax.rem(jnp.minimum(j, n_items - 1), nq)
    lam = lam_ref[0]
    lane = lax.broadcasted_iota(jnp.int32, (1, V_HEAD_DIM), 1)
    first_half = lane < HEAD_DIM
    nt_dims = (((1,), (1,)), ((), ()))

    @pl.when(j == 0)
    def _():
        sb_ref[...] = jnp.zeros_like(sb_ref)
        mb_ref[...] = jnp.zeros_like(mb_ref)

    @pl.when(qb == 0)
    def _():
        k = k_ref[0]
        ak = ak_ref[0]
        kt_ref[0] = jnp.where(first_half, k, ak)
        kt_ref[1] = jnp.where(first_half, ak, k)

    @pl.when(lax.rem(jnp.maximum(j - 1, 0), nq) == 0)
    def _():
        pad_row = lax.broadcasted_iota(jnp.int32, (PV_ROWS - V_HEAD_DIM, 1), 0)
        ones_row = jnp.where(pad_row == 0, 1.0, 0.0).astype(BF16)
        for kb in range(nkb):
            vt_ref[kb, :V_HEAD_DIM, :] = v_ref[0, kb * tk:(kb + 1) * tk, :].astype(F32).T.astype(BF16)
            vt_ref[kb, V_HEAD_DIM:, :] = jnp.broadcast_to(ones_row, (PV_ROWS - V_HEAD_DIM, tk))

    def step(s_new, m_new, s_old, m_old):
        q = q_ref[0]
        aq = aq_ref[0]
        naq = -aq
        for c, keep in enumerate((first_half, jnp.logical_not(first_half))):
            ql_ref[c, 0] = jnp.where(keep, q, aq)
            ql_ref[c, 1] = jnp.where(keep, q, naq)
        col_max = [jnp.max(m_old[c], axis=0, keepdims=True) for c in range(2)]

        def key_block(r):
            first = r == 0
            shared = r < diag_blocks
            start = qb * diag_blocks + r
            wrapped = start >= nkb
            kb = jnp.where(wrapped, start - nkb, start)
            side = 0 if shared else jnp.where(wrapped, 0, 1)
            for c in range(2):
                p = jnp.exp2(s_old[c, r] - col_max[c]).astype(BF16)
                part = jnp.dot(vt_ref[r], p, preferred_element_type=F32)
                acc_ref[c] = part if first else acc_ref[c] + part
                s = lax.dot_general(kt_ref[c, pl.ds(pl.multiple_of(kb * tk, tk), tk), :], ql_ref[c, side], nt_dims,
                                    preferred_element_type=F32)
                if shared:
                    s = s + dg_ref[0, r * tk:(r + 1) * tk, :]
                s_new[c, kb] = s
                bm = s[0:8]
                for i in range(8, tk, 8):
                    bm = jnp.maximum(bm, s[i:i + 8])
                m_new[c] = bm if first else jnp.maximum(m_new[c], bm)

        for r in range(nkb):
            key_block(r)

        outs = [acc_ref[c, :V_HEAD_DIM, :] / acc_ref[c, V_HEAD_DIM:V_HEAD_DIM + 1, :] for c in range(2)]
        a = (outs[0] - lam * outs[1]).T
        ms = jnp.mean(a * a, axis=-1, keepdims=True)
        y = a * lax.rsqrt(ms + EPS) * g_ref[...] * (1.0 - lam_init)
        o_ref[0] = y.astype(BF16)

    @pl.when(lax.rem(j, 2) == 0)
    def _():
        step(sa_ref, ma_ref, sb_ref, mb_ref)

    @pl.when(lax.rem(j, 2) == 1)
    def _():
        step(sb_ref, mb_ref, sa_ref, ma_ref)


def _attention(proj3, lam, subln_g, lam_init):
    b, s, _ = proj3.shape
    nh = N_HEADS
    t = ATTN_TILE
    nq = s // t
    per_head = b * nq
    n_items = nh * per_head
    augq_np, augk_np, dg_np = _alibi_tables()
    kern = functools.partial(_attn_kernel, lam_init=lam_init, n_items=n_items)
    smem = pl.BlockSpec(memory_space=pltpu.SMEM)
    cur = lambda j: jnp.minimum(j, n_items - 1)
    prev = lambda j: jnp.maximum(j - 1, 0)
    head = lambda i: i // per_head
    batch = lambda i: (i % per_head) // nq
    qblk = lambda i: i % nq
    nkb = s // ATTN_KEYS
    score_scratch = pltpu.VMEM((2, nkb, ATTN_KEYS, t), F32)
    max_scratch = pltpu.VMEM((2, 8, t), F32)
    return pl.pallas_call(
        kern,
        grid=(n_items + 1,),
        in_specs=[smem,
                  pl.BlockSpec((1, t, V_HEAD_DIM), lambda j: (batch(cur(j)), qblk(cur(j)), head(cur(j)))),
                  pl.BlockSpec((1, s, V_HEAD_DIM), lambda j: (batch(cur(j)), 0, nh + head(cur(j)))),
                  pl.BlockSpec((1, s, V_HEAD_DIM), lambda j: (batch(prev(j)), 0, 2 * nh + head(prev(j)))),
                  pl.BlockSpec((1, t, V_HEAD_DIM), lambda j: (head(cur(j)), qblk(cur(j)), 0)),
                  pl.BlockSpec((1, s, V_HEAD_DIM), lambda j: (head(cur(j)), 0, 0)),
                  pl.BlockSpec((1, t, t), lambda j: (head(cur(j)), 0, 0), pipeline_mode=pl.Buffered(1)),
                  pl.BlockSpec((1, V_HEAD_DIM), lambda j: (0, 0))],
        out_specs=pl.BlockSpec((1, t, V_HEAD_DIM), lambda j: (batch(prev(j)), qblk(prev(j)), head(prev(j)))),
        out_shape=jax.ShapeDtypeStruct((b, s, ATTN_WIDTH), BF16),
        scratch_shapes=[pltpu.VMEM((2, s, V_HEAD_DIM), BF16), score_scratch, score_scratch,
                        max_scratch, max_scratch, pltpu.VMEM((nkb, PV_ROWS, ATTN_KEYS), BF16),
                        pltpu.VMEM((2, 2, t, V_HEAD_DIM), BF16), pltpu.VMEM((2, PV_ROWS, t), F32)],
        compiler_params=_params(("arbitrary",), 56),
        name="diff_attn",
    )(lam, proj3, proj3, proj3, jnp.asarray(augq_np), jnp.asarray(augk_np), jnp.asarray(dg_np), subln_g)


def _filter_kernel(z_ref, w1_ref, b1_ref, fr_ref, w2_ref, b2_ref, w3_ref, dl_ref, o_ref, *, tr):
    hi = lax.Precision.HIGHEST
    fr = fr_ref[...]
    h = jnp.sin(fr * (jnp.dot(z_ref[...], w1_ref[...], precision=hi, preferred_element_type=F32) + b1_ref[...]))
    h = jnp.sin(fr * (jnp.dot(h, w2_ref[...], precision=hi, preferred_element_type=F32) + b2_ref[...]))
    h = jnp.dot(h.astype(BF16), w3_ref[...].astype(BF16), preferred_element_type=F32)
    row = pl.program_id(0) * tr + lax.broadcasted_iota(jnp.int32, (tr, 1), 0)
    t = row.astype(F32) * (1.0 / (SEQ - 1))
    decay = jnp.exp(-t * dl_ref[...]) + DECAY_SHIFT
    c = HYENA_WIDTH
    o_ref[:, :c] = (h[:, :c] * decay).astype(BF16)
    o_ref[:, c:] = jnp.where(row == 0, 0.0, h[:, c:] * decay).astype(BF16)


def _filters(zpad, w1pad, b1, freq, w2, b2, w3, deltas, tr=256):
    L = SEQ
    c2 = 2 * HYENA_WIDTH
    full = lambda shape: pl.BlockSpec(shape, lambda i: (0, 0))
    return pl.pallas_call(
        functools.partial(_filter_kernel, tr=tr),
        grid=(L // tr,),
        in_specs=[pl.BlockSpec((tr, LANE), lambda i: (i, 0)),
                  full((LANE, FILTER_HIDDEN)), full((1, FILTER_HIDDEN)), full((1, FILTER_HIDDEN)),
                  full((FILTER_HIDDEN, FILTER_HIDDEN)), full((1, FILTER_HIDDEN)),
                  full((FILTER_HIDDEN, c2)), full((1, HYENA_WIDTH))],
        out_specs=pl.BlockSpec((tr, c2), lambda i: (i, 0)),
        out_shape=jax.ShapeDtypeStruct((L, c2), BF16),
        compiler_params=_params(("parallel",), 48),
        name="hyena_filters",
    )(zpad, w1pad, b1, freq, w2, b2, w3, deltas)


def _kspec_kernel(f_ref, h_ref, o_ref, fb_ref, g_ref):
    c = HYENA_WIDTH
    t = DFT_TILE
    L = SEQ
    fb = f_ref[0].astype(BF16)
    fb_ref[0] = fb
    p = jnp.dot(fb, h_ref[...], preferred_element_type=F32)
    row = pl.program_id(0) * t + lax.broadcasted_iota(jnp.int32, (t, 1), 0)
    o_ref[0] = p[:t, :c] + p[:t, c:]
    o_ref[1] = jnp.where(row == 0, p[t:, :c] + p[t:, c:], p[t:, :c] - p[t:, c:])
    col = lax.broadcasted_iota(jnp.int32, (1, L), 1)
    sign = (1 - 2 * (row & 1)).astype(F32)
    gc = f_ref[0, :t, :] * jnp.where(col == 0, 1.0 / FFT_N, 2.0 / FFT_N)
    gs = jnp.where(col == 0, sign * (1.0 / FFT_N),
                   jnp.where(row == 0, 0.0, f_ref[0, t:, :] * (2.0 / FFT_N)))
    g_ref[:, :L] = gc.astype(BF16)
    g_ref[:, L:] = gs.astype(BF16)


def _kspec(table, hcat):
    L = SEQ
    c = HYENA_WIDTH
    t = DFT_TILE
    return pl.pallas_call(
        _kspec_kernel,
        grid=(L // t,),
        in_specs=[pl.BlockSpec((1, 2 * t, L), lambda i: (i, 0, 0)),
                  pl.BlockSpec((L, 2 * c), lambda i: (0, 0), pipeline_mode=pl.Buffered(1))],
        out_specs=[pl.BlockSpec((2, t, c), lambda i: (0, i, 0)),
                   pl.BlockSpec((1, 2 * t, L), lambda i: (i, 0, 0)),
                   pl.BlockSpec((t, 2 * L), lambda i: (i, 0))],
        out_shape=[jax.ShapeDtypeStruct((2, L, c), F32),
                   jax.ShapeDtypeStruct(table.shape, BF16),
                   jax.ShapeDtypeStruct((L, 2 * L), BF16)],
        compiler_params=_params(("arbitrary",), 56),
        name="hyena_kspec",
    )(table, hcat)


def _fwd_kernel(f_ref, vg_ref, k_ref, y_ref):
    t = DFT_TILE
    for bb in range(vg_ref.shape[0]):
        vg = vg_ref[bb]
        for r0 in range(0, t, ROW_CHUNK):
            xr = jnp.dot(f_ref[0, pl.ds(r0, ROW_CHUNK), :], vg, preferred_element_type=F32)
            xi = jnp.dot(f_ref[0, pl.ds(t + r0, ROW_CHUNK), :], vg, preferred_element_type=F32)
            kr = k_ref[0, pl.ds(r0, ROW_CHUNK), :]
            ki = k_ref[1, pl.ds(r0, ROW_CHUNK), :]
            row = pl.program_id(0) * t + r0 + lax.broadcasted_iota(jnp.int32, (ROW_CHUNK, 1), 0)
            first = row == 0
            yr = jnp.where(first, xr * kr, xr * kr - xi * ki)
            yi = jnp.where(first, xi * ki, xr * ki + xi * kr)
            y_ref[bb, 0, pl.ds(r0, ROW_CHUNK), :] = yr.astype(BF16)
            y_ref[bb, 1, pl.ds(r0, ROW_CHUNK), :] = yi.astype(BF16)


def _fwd_dft(table_bf16, vg, kspec, nb=2):
    b, L, c = vg.shape
    t = DFT_TILE
    return pl.pallas_call(
        _fwd_kernel,
        grid=(L // t, b // nb),
        in_specs=[pl.BlockSpec((1, 2 * t, L), lambda j, bi: (j, 0, 0)),
                  pl.BlockSpec((nb, L, c), lambda j, bi: (bi, 0, 0)),
                  pl.BlockSpec((2, t, c), lambda j, bi: (0, j, 0))],
        out_specs=pl.BlockSpec((nb, 2, t, c), lambda j, bi: (bi, 0, j, 0)),
        out_shape=jax.ShapeDtypeStruct((b, 2, L, c), BF16),
        compiler_params=_params(("arbitrary", "arbitrary"), 56),
        name="hyena_fwd_dft",
    )(table_bf16, vg, kspec)


def _inv_kernel(g_ref, y_ref, vg_ref, x1_ref, d_ref, gain_ref, o_ref):
    for r0 in range(0, o_ref.shape[1], ROW_CHUNK):
        rows = pl.ds(r0, ROW_CHUNK)
        y = jnp.dot(g_ref[rows, :], y_ref[0], preferred_element_type=F32)
        vg = vg_ref[0, rows, :].astype(F32)
        y = (y + vg * d_ref[...]) * x1_ref[0, rows, :].astype(F32)
        ms = jnp.mean(y * y, axis=-1, keepdims=True)
        o_ref[0, rows, :] = (y * lax.rsqrt(ms + EPS) * gain_ref[...]).astype(BF16)


def _inv_dft(inv_weights, yspec, vg, x1, d_skip, gain, tt=1024):
    b, L, c = vg.shape
    return pl.pallas_call(
        _inv_kernel,
        grid=(L // tt, b),
        in_specs=[pl.BlockSpec((tt, 2 * L), lambda j, bi: (j, 0)),
                  pl.BlockSpec((1, 2 * L, c), lambda j, bi: (bi, 0, 0)),
                  pl.BlockSpec((1, tt, c), lambda j, bi: (bi, j, 0)),
                  pl.BlockSpec((1, tt, c), lambda j, bi: (bi, j, 0)),
                  pl.BlockSpec((1, c), lambda j, bi: (0, 0)),
                  pl.BlockSpec((1, c), lambda j, bi: (0, 0))],
        out_specs=pl.BlockSpec((1, tt, c), lambda j, bi: (bi, j, 0)),
        out_shape=jax.ShapeDtypeStruct((b, L, c), BF16),
        compiler_params=_params(("arbitrary", "arbitrary"), 56),
        name="hyena_inv_dft",
    )(inv_weights, yspec, vg, x1, d_skip, gain)


def _layer_norm(y, g, b):
    mu = jnp.mean(y, axis=-1, keepdims=True)
    yc = y - mu
    var = jnp.mean(yc * yc, axis=-1, keepdims=True)
    return yc * lax.rsqrt(var + EPS) * g + b


def _outproj_kernel(att_ref, hy_ref, w_ref, x_ref, g_ref, b_ref, o_ref):
    a = ATTN_WIDTH
    for r0 in range(0, o_ref.shape[0], ROW_CHUNK):
        rows = pl.ds(r0, ROW_CHUNK)
        mix = jnp.dot(att_ref[rows, :], w_ref[:a, :], preferred_element_type=F32)
        mix = mix + jnp.dot(hy_ref[rows, :], w_ref[a:, :], preferred_element_type=F32)
        y = ALPHA * x_ref[rows, :] + mix
        o_ref[rows, :] = _layer_norm(y, g_ref[...], b_ref[...])


def _outproj(att2d, hy2d, w_bf16, x2d, g, b, tm=1024):
    m, d = x2d.shape
    return pl.pallas_call(
        _outproj_kernel,
        grid=(m // tm,),
        in_specs=[pl.BlockSpec((tm, ATTN_WIDTH), lambda i: (i, 0)),
                  pl.BlockSpec((tm, HYENA_WIDTH), lambda i: (i, 0)),
                  pl.BlockSpec((d, d), lambda i: (0, 0), pipeline_mode=pl.Buffered(1)),
                  pl.BlockSpec((tm, d), lambda i: (i, 0)),
                  pl.BlockSpec((1, d), lambda i: (0, 0)),
                  pl.BlockSpec((1, d), lambda i: (0, 0))],
        out_specs=pl.BlockSpec((tm, d), lambda i: (i, 0)),
        out_shape=jax.ShapeDtypeStruct((m, d), F32),
        compiler_params=_params(("parallel",), 56),
        name="outproj_ln",
    )(att2d, hy2d, w_bf16, x2d, g, b)


def _ffn_kernel(x_ref, w1_ref, w2_ref, g_ref, b_ref, o_ref, xb_ref):
    f = pl.program_id(1)
    last = pl.num_programs(1) - 1
    tm = o_ref.shape[0]

    def mlp(xb, base):
        h = jnp.dot(xb, w1_ref[...], preferred_element_type=F32)
        h = jnp.square(jnp.maximum(h, 0.0)).astype(BF16)
        return base + jnp.dot(h, w2_ref[...], preferred_element_type=F32)

    @pl.when(f == 0)
    def _():
        for r0 in range(0, tm, ROW_CHUNK):
            rows = pl.ds(r0, ROW_CHUNK)
            x = x_ref[rows, :]
            xb = x.astype(BF16)
            xb_ref[rows, :] = xb
            o_ref[rows, :] = mlp(xb, ALPHA * x)

    @pl.when(jnp.logical_and(f > 0, f < last))
    def _():
        for r0 in range(0, tm, 2 * ROW_CHUNK):
            rows = pl.ds(r0, 2 * ROW_CHUNK)
            o_ref[rows, :] = mlp(xb_ref[rows, :], o_ref[rows, :])

    @pl.when(f == last)
    def _():
        for r0 in range(0, tm, ROW_CHUNK):
            rows = pl.ds(r0, ROW_CHUNK)
            y = mlp(xb_ref[rows, :], o_ref[rows, :])
            o_ref[rows, :] = _layer_norm(y, g_ref[...], b_ref[...])


def _ffn(x2d, w1_bf16, w2_bf16, g, b, tm=1024, tf=FFN_TILE):
    m, d = x2d.shape
    dff = w1_bf16.shape[1]
    return pl.pallas_call(
        _ffn_kernel,
        grid=(m // tm, dff // tf),
        in_specs=[pl.BlockSpec((tm, d), lambda i, j: (i, 0)),
                  pl.BlockSpec((d, tf), lambda i, j: (0, j)),
                  pl.BlockSpec((tf, d), lambda i, j: (j, 0)),
                  pl.BlockSpec((1, d), lambda i, j: (0, 0)),
                  pl.BlockSpec((1, d), lambda i, j: (0, 0))],
        out_specs=pl.BlockSpec((tm, d), lambda i, j: (i, 0)),
        out_shape=jax.ShapeDtypeStruct((m, d), F32),
        scratch_shapes=[pltpu.VMEM((tm, d), BF16)],
        compiler_params=_params(("parallel", "arbitrary"), 58),
        name="ffn_ln",
    )(x2d, w1_bf16, w2_bf16, g, b)


def kernel(x, w_in, lambda_q1, lambda_k1, lambda_q2, lambda_k2, subln_g, conv_w, conv_b, filt_w1, filt_b1,
           filt_freq, filt_w2, filt_b2, filt_w3, hyena_skip, hyena_gain, w_out, ln1_g, ln1_b, w_ff1, w_ff2,
           ln2_g, ln2_b):
    B, S, D = x.shape
    assert (B, S, D) == (BATCH, SEQ, D_MODEL)
    zpad_np, deltas_np = _filter_tables()
    dft_tab = jnp.asarray(_dft_table())
    zpad = jnp.asarray(zpad_np)
    deltas = jnp.asarray(deltas_np)
    row = lambda v: v.astype(F32).reshape(1, -1)

    x2d = x.reshape(B * S, D)
    for l in range(DEPTH):
        lam_init = 0.8 - 0.6 * math.exp(-0.3 * l)
        lam = (jnp.exp(jnp.sum(lambda_q1[l].astype(F32) * lambda_k1[l].astype(F32)))
               - jnp.exp(jnp.sum(lambda_q2[l].astype(F32) * lambda_k2[l].astype(F32)))
               + lam_init).reshape(1)

        proj, vg, x1, w_out_b, w_ff1_b, w_ff2_b = _inproj(
            x2d, w_in[l].astype(BF16), conv_w[l].astype(F32), row(conv_b[l]),
            [w_out[l].astype(F32), w_ff1[l].astype(F32), w_ff2[l].astype(F32)])
        proj3 = proj.reshape(B, S, 3 * ATTN_WIDTH)
        vg = vg.reshape(B, S, HYENA_WIDTH)
        x1 = x1.reshape(B, S, HYENA_WIDTH)

        att = _attention(proj3, lam, row(subln_g[l]), lam_init)

        w1pad = jnp.zeros((LANE, FILTER_HIDDEN), F32).at[:FILTER_EMB].set(filt_w1[l].astype(F32))
        hcat = _filters(zpad, w1pad, row(filt_b1[l]), row(filt_freq[l]), filt_w2[l].astype(F32),
                        row(filt_b2[l]), filt_w3[l].astype(F32), deltas)
        kspec, dft_tab_b, inv_weights = _kspec(dft_tab, hcat)
        yspec = _fwd_dft(dft_tab_b, vg, kspec)
        hy = _inv_dft(inv_weights, yspec.reshape(B, 2 * S, HYENA_WIDTH), vg, x1, row(hyena_skip[l]),
                      row(hyena_gain[l]))

        x2d = _outproj(att.reshape(B * S, ATTN_WIDTH), hy.reshape(B * S, HYENA_WIDTH), w_out_b,
                       x2d, row(ln1_g[l]), row(ln1_b[l]))
        x2d = _ffn(x2d, w_ff1_b, w_ff2_b, row(ln2_g[l]), row(ln2_b[l]))
    return x2d.reshape(B, S, D)
```

```python
import functools
import math

import numpy as np
import jax
import jax.numpy as jnp
from jax import lax
from jax.experimental import pallas as pl
from jax.experimental.pallas import tpu as pltpu

D_MODEL = 2048
BATCH = 8
SEQ = 2048
DEPTH = 1
ATTN_WIDTH = D_MODEL // 2
HYENA_WIDTH = D_MODEL - ATTN_WIDTH
N_HEADS = 8
HEAD_DIM = ATTN_WIDTH // N_HEADS // 2
V_HEAD_DIM = 2 * HEAD_DIM
SHORT_CONV = 3
FILTER_EMB = 33
FILTER_HIDDEN = 64
DECAY_FAST = 0.3
DECAY_SLOW = 1.5
DECAY_TARGET = 1e-2
DECAY_SHIFT = 0.0
D_FF = 4 * D_MODEL
ALPHA = (2.0 * DEPTH) ** 0.25
EPS = 1e-5
IN_COLS = 3 * ATTN_WIDTH + 3 * HYENA_WIDTH
FFT_N = 2 * SEQ
LANE = 128
MXU_DIM = 256
DFT_TILE = 512
FFN_TILE = 1024
ROW_CHUNK = 256
ATTN_TILE = 512
ATTN_KEYS = 512
HALO = 16
ATTN_COL_TILES = 2
PV_ROWS = V_HEAD_DIM + 16
LOG2E = math.log2(math.e)
Q_PRESCALE = HEAD_DIM ** -0.5 * LOG2E

BF16 = jnp.bfloat16
F32 = jnp.float32

_MIB = 1024 * 1024


def _params(semantics, vmem_mib):
    return pltpu.CompilerParams(dimension_semantics=semantics, vmem_limit_bytes=vmem_mib * _MIB)


@functools.lru_cache(maxsize=None)
def _dft_table():
    k = np.arange(SEQ, dtype=np.int64)
    phase = (k[:, None] * k[None, :]) % FFT_N
    ang = (2.0 * np.pi / FFT_N) * phase
    c = np.cos(ang)
    s = -np.sin(ang)
    s[0, :] = np.where(k % 2 == 0, 1.0, -1.0)
    nt = SEQ // DFT_TILE
    tiled = np.concatenate([c.reshape(nt, DFT_TILE, SEQ), s.reshape(nt, DFT_TILE, SEQ)], axis=1)
    return tiled.astype(np.float32)


@functools.lru_cache(maxsize=None)
def _filter_tables():
    L = SEQ
    t = np.linspace(0.0, 1.0, L)[:, None]
    bands = (FILTER_EMB - 1) // 2
    w = 2.0 * np.pi * np.arange(L, dtype=np.float64)[:, None] / L
    f = np.linspace(1e-4, bands - 1, bands)[None, :]
    z = np.concatenate([t, np.cos(f * w), -np.sin(f * w)], axis=-1)
    zpad = np.zeros((L, LANE), np.float32)
    zpad[:, :FILTER_EMB] = z
    max_decay = math.log(DECAY_TARGET) / DECAY_FAST
    min_decay = math.log(DECAY_TARGET) / DECAY_SLOW
    deltas = np.abs(np.linspace(min_decay, max_decay, HYENA_WIDTH))[None, :].astype(np.float32)
    return zpad, deltas


def _inproj_kernel(x_ref, xp_ref, xn_ref, wa_ref, wb_ref, wc_ref, sa_ref, sb_ref, sc_ref,
                   cwa_ref, cwb_ref, cwc_ref, cba_ref, cbb_ref, cbc_ref, ea_ref, eb_ref, ec_ref,
                   o_ref, vg_ref, x1_ref, ea_out, eb_out, ec_out, xe_ref):
    i = pl.program_id(0)
    j = pl.program_id(1)
    tm = x_ref.shape[0]
    tc = wa_ref.shape[1]
    tiles_per_seq = SEQ // tm
    ea_out[...] = ea_ref[...].astype(BF16)
    eb_out[...] = eb_ref[...].astype(BF16)
    ec_out[...] = ec_ref[...].astype(BF16)

    @pl.when(j == 0)
    def _():
        pos = lax.rem(i, tiles_per_seq)
        xe_ref[:HALO, :] = jnp.where(pos == 0, 0.0, xp_ref[...]).astype(BF16)
        xe_ref[HALO:HALO + tm, :] = x_ref[...].astype(BF16)
        xe_ref[HALO + tm:, :] = jnp.where(pos == tiles_per_seq - 1, 0.0, xn_ref[...]).astype(BF16)

    @pl.when(j < ATTN_COL_TILES)
    def _():
        xb = xe_ref[HALO:HALO + tm, :]
        for p, (w_ref, s_ref) in enumerate(((wa_ref, sa_ref), (wb_ref, sb_ref), (wc_ref, sc_ref))):
            acc = jnp.dot(xb, w_ref[...], preferred_element_type=F32)
            o_ref[:, p * tc:(p + 1) * tc] = (acc * s_ref[...]).astype(BF16)

    @pl.when(j >= ATTN_COL_TILES)
    def _():
        xe = xe_ref[...]
        rows = xe.shape[0]
        for c0 in range(0, tc, MXU_DIM):
            cols = slice(c0, c0 + MXU_DIM)
            z = []
            for w_ref, cw_ref, cb_ref in ((wa_ref, cwa_ref, cba_ref), (wb_ref, cwb_ref, cbb_ref),
                                          (wc_ref, cwc_ref, cbc_ref)):
                u = jnp.dot(xe, w_ref[:, cols], preferred_element_type=F32)
                cw = cw_ref[:, cols]
                zz = (cb_ref[:, cols] + pltpu.roll(u, 1, 0) * cw[0:1] + u * cw[1:2]
                      + pltpu.roll(u, rows - 1, 0) * cw[2:3])
                z.append(zz[HALO:HALO + tm])
            x1, x2, v = z
            vg_ref[:, cols] = (v * x2).astype(BF16)
            x1_ref[:, cols] = x1.astype(BF16)


def _inproj(x2d, w_bf16, col_scale, conv_w, conv_b, side_weights, tm=1024, tc=512):
    m, k = x2d.shape
    attn_cols = 3 * ATTN_WIDTH
    na = ATTN_COL_TILES
    assert attn_cols == na * 3 * tc
    nh_tiles = HYENA_WIDTH // tc
    nj = na + nh_tiles
    steps = (m // tm) * nj
    halo_per_tile = tm // HALO

    def col_block(p):
        return lambda i, j: (0, jnp.where(j < na, 3 * j + p, 3 * na + nh_tiles * p + (j - na)))

    def conv_block(p):
        return lambda i, j: (0, nh_tiles * p + jnp.maximum(j - na, 0))

    def slab(w):
        return pl.BlockSpec((w.shape[0] // steps, w.shape[1]), lambda i, j: (i * nj + j, 0))

    side_specs = [slab(w) for w in side_weights]
    hy_out = pl.BlockSpec((tm, tc), lambda i, j: (i, jnp.maximum(j - na, 0)))
    return pl.pallas_call(
        _inproj_kernel,
        grid=(m // tm, nj),
        in_specs=[pl.BlockSpec((tm, k), lambda i, j: (i, 0)),
                  pl.BlockSpec((HALO, k), lambda i, j: (jnp.maximum(i * halo_per_tile - 1, 0), 0)),
                  pl.BlockSpec((HALO, k), lambda i, j: (jnp.minimum((i + 1) * halo_per_tile, m // HALO - 1), 0))]
                 + [pl.BlockSpec((k, tc), col_block(p)) for p in range(3)]
                 + [pl.BlockSpec((1, tc), col_block(p)) for p in range(3)]
                 + [pl.BlockSpec((SHORT_CONV, tc), conv_block(p)) for p in range(3)]
                 + [pl.BlockSpec((1, tc), conv_block(p)) for p in range(3)]
                 + side_specs,
        out_specs=[pl.BlockSpec((tm, 3 * tc), lambda i, j: (i, jnp.minimum(j, na - 1))), hy_out, hy_out] + side_specs,
        out_shape=[jax.ShapeDtypeStruct((m, attn_cols), BF16),
                   jax.ShapeDtypeStruct((m, HYENA_WIDTH), BF16),
                   jax.ShapeDtypeStruct((m, HYENA_WIDTH), BF16)]
                  + [jax.ShapeDtypeStruct(w.shape, BF16) for w in side_weights],
        scratch_shapes=[pltpu.VMEM((tm + 2 * HALO, k), BF16)],
        compiler_params=_params(("arbitrary", "arbitrary"), 58),
        name="inproj",
    )(x2d, x2d, x2d, w_bf16, w_bf16, w_bf16, col_scale, col_scale, col_scale,
      conv_w, conv_w, conv_w, conv_b, conv_b, conv_b, *side_weights)


@functools.lru_cache(maxsize=None)
def _alibi_tables():
    pos = np.arange(SEQ)
    hi = ((pos >> 8) << 8).astype(np.float64)
    lo = (pos & 255).astype(np.float64)
    augq = np.zeros((N_HEADS, SEQ, V_HEAD_DIM), np.float64)
    augk = np.zeros((N_HEADS, SEQ, V_HEAD_DIM), np.float64)
    t = ATTN_TILE
    ahead = np.maximum(np.arange(t)[:, None] - np.arange(t)[None, :], 0).astype(np.float64)
    dg = np.zeros((N_HEADS, t, t), np.float64)
    for h in range(N_HEADS):
        c = 2.0 ** (-8.0 * (h + 1) / N_HEADS) * LOG2E
        rest = c
        for p in range(3):
            piece = float(np.float32(rest).astype(BF16).astype(np.float64))
            rest -= piece
            for base in (0, HEAD_DIM):
                augq[h, :, base + 2 * p] = piece
                augq[h, :, base + 2 * p + 1] = piece
                augk[h, :, base + 2 * p] = hi
                augk[h, :, base + 2 * p + 1] = lo
                augq[h, :, base + 6 + 2 * p] = -hi
                augq[h, :, base + 6 + 2 * p + 1] = -lo
                augk[h, :, base + 6 + 2 * p] = piece
                augk[h, :, base + 6 + 2 * p + 1] = piece
        dg[h] = -2.0 * c * ahead
    return augq.astype(BF16), augk.astype(BF16), dg.astype(np.float32)


def _attn_kernel(lam_ref, q_ref, k_ref, v_ref, aq_ref, ak_ref, dg_ref, g_ref, o_ref,
                 kt_ref, sa_ref, sb_ref, ma_ref, mb_ref, vt_ref, ql_ref, acc_ref, *, lam_init, n_items):
    t = ATTN_TILE
    tk = ATTN_KEYS
    nkb = SEQ // tk
    nq = SEQ // t
    diag_blocks = t // tk
    j = pl.program_id(0)
    qb = lax.rem(jnp.minimum(j, n_items - 1), nq)
    lam = lam_ref[0]
    lane = lax.broadcasted_iota(jnp.int32, (1, V_HEAD_DIM), 1)
    first_half = lane < HEAD_DIM
    nt_dims = (((1,), (1,)), ((), ()))

    @pl.when(j == 0)
    def _():
        sb_ref[...] = jnp.zeros_like(sb_ref)
        mb_ref[...] = jnp.zeros_like(mb_ref)

    @pl.when(qb == 0)
    def _():
        k = k_ref[0]
        ak = ak_ref[0]
        kt_ref[0] = jnp.where(first_half, k, ak)
        kt_ref[1] = jnp.where(first_half, ak, k)

    @pl.when(lax.rem(jnp.maximum(j - 1, 0), nq) == 0)
    def _():
        pad_row = lax.broadcasted_iota(jnp.int32, (PV_ROWS - V_HEAD_DIM, 1), 0)
        ones_row = jnp.where(pad_row == 0, 1.0, 0.0).astype(BF16)
        for kb in range(nkb):
            vt_ref[kb, :V_HEAD_DIM, :] = v_ref[0, kb * tk:(kb + 1) * tk, :].astype(F32).T.astype(BF16)
            vt_ref[kb, V_HEAD_DIM:, :] = jnp.broadcast_to(ones_row, (PV_ROWS - V_HEAD_DIM, tk))

    def step(s_new, m_new, s_old, m_old):
        q = q_ref[0]
        aq = aq_ref[0, pl.ds(pl.multiple_of(qb * t, t), t), :]
        naq = -aq
        for c, keep in enumerate((first_half, jnp.logical_not(first_half))):
            ql_ref[c, 0] = jnp.where(keep, q, aq)
            ql_ref[c, 1] = jnp.where(keep, q, naq)
        col_max = [jnp.max(m_old[c], axis=0, keepdims=True) for c in range(2)]

        def key_block(r):
            first = r == 0
            shared = r < diag_blocks
            start = qb * diag_blocks + r
            wrapped = start >= nkb
            kb = jnp.where(wrapped, start - nkb, start)
            side = 0 if shared else jnp.where(wrapped, 0, 1)
            for c in range(2):
                p = jnp.exp2(s_old[c, r] - col_max[c]).astype(BF16)
                part = jnp.dot(vt_ref[r], p, preferred_element_type=F32)
                acc_ref[c] = part if first else acc_ref[c] + part
                s = lax.dot_general(kt_ref[c, pl.ds(pl.multiple_of(kb * tk, tk), tk), :], ql_ref[c, side], nt_dims,
                                    preferred_element_type=F32)
                if shared:
                    s = s + dg_ref[0, r * tk:(r + 1) * tk, :]
                s_new[c, kb] = s
                bm = s[0:8]
                for i in range(8, tk, 8):
                    bm = jnp.maximum(bm, s[i:i + 8])
                m_new[c] = bm if first else jnp.maximum(m_new[c], bm)

        for r in range(nkb):
            key_block(r)

        outs = [acc_ref[c, :V_HEAD_DIM, :] / acc_ref[c, V_HEAD_DIM:V_HEAD_DIM + 1, :] for c in range(2)]
        a = (outs[0] - lam * outs[1]).T
        ms = jnp.mean(a * a, axis=-1, keepdims=True)
        y = a * lax.rsqrt(ms + EPS) * g_ref[...] * (1.0 - lam_init)
        o_ref[0] = y.astype(BF16)

    @pl.when(lax.rem(j, 2) == 0)
    def _():
        step(sa_ref, ma_ref, sb_ref, mb_ref)

    @pl.when(lax.rem(j, 2) == 1)
    def _():
        step(sb_ref, mb_ref, sa_ref, ma_ref)


def _attention(proj3, lam, subln_g, lam_init):
    b, s, _ = proj3.shape
    nh = N_HEADS
    t = ATTN_TILE
    nq = s // t
    per_head = b * nq
    n_items = nh * per_head
    augq_np, augk_np, dg_np = _alibi_tables()
    kern = functools.partial(_attn_kernel, lam_init=lam_init, n_items=n_items)
    smem = pl.BlockSpec(memory_space=pltpu.SMEM)
    cur = lambda j: jnp.minimum(j, n_items - 1)
    prev = lambda j: jnp.maximum(j - 1, 0)
    head = lambda i: i // per_head
    batch = lambda i: (i % per_head) // nq
    qblk = lambda i: i % nq
    nkb = s // ATTN_KEYS
    score_scratch = pltpu.VMEM((2, nkb, ATTN_KEYS, t), F32)
    max_scratch = pltpu.VMEM((2, 8, t), F32)
    return pl.pallas_call(
        kern,
        grid=(n_items + 1,),
        in_specs=[smem,
                  pl.BlockSpec((1, t, V_HEAD_DIM), lambda j: (batch(cur(j)), qblk(cur(j)), head(cur(j)))),
                  pl.BlockSpec((1, s, V_HEAD_DIM), lambda j: (batch(cur(j)), 0, nh + head(cur(j)))),
                  pl.BlockSpec((1, s, V_HEAD_DIM), lambda j: (batch(prev(j)), 0, 2 * nh + head(prev(j)))),
                  pl.BlockSpec((1, s, V_HEAD_DIM), lambda j: (head(cur(j)), 0, 0)),
                  pl.BlockSpec((1, s, V_HEAD_DIM), lambda j: (head(cur(j)), 0, 0)),
                  pl.BlockSpec((1, t, t), lambda j: (head(cur(j)), 0, 0), pipeline_mode=pl.Buffered(1)),
                  pl.BlockSpec((1, V_HEAD_DIM), lambda j: (0, 0))],
        out_specs=pl.BlockSpec((1, t, V_HEAD_DIM), lambda j: (batch(prev(j)), qblk(prev(j)), head(prev(j)))),
        out_shape=jax.ShapeDtypeStruct((b, s, ATTN_WIDTH), BF16),
        scratch_shapes=[pltpu.VMEM((2, s, V_HEAD_DIM), BF16), score_scratch, score_scratch,
                        max_scratch, max_scratch, pltpu.VMEM((nkb, PV_ROWS, ATTN_KEYS), BF16),
                        pltpu.VMEM((2, 2, t, V_HEAD_DIM), BF16), pltpu.VMEM((2, PV_ROWS, t), F32)],
        compiler_params=_params(("arbitrary",), 56),
        name="diff_attn",
    )(lam, proj3, proj3, proj3, jnp.asarray(augq_np), jnp.asarray(augk_np), jnp.asarray(dg_np), subln_g)


def _filter_kernel(z_ref, w1_ref, b1_ref, fr_ref, w2_ref, b2_ref, w3_ref, dl_ref, o_ref, *, tr):
    hi = lax.Precision.HIGHEST
    fr = fr_ref[...]
    h = jnp.sin(fr * (jnp.dot(z_ref[...], w1_ref[...], precision=hi, preferred_element_type=F32) + b1_ref[...]))
    h = jnp.sin(fr * (jnp.dot(h, w2_ref[...], precision=hi, preferred_element_type=F32) + b2_ref[...]))
    h = jnp.dot(h.astype(BF16), w3_ref[...].astype(BF16), preferred_element_type=F32)
    row = pl.program_id(0) * tr + lax.broadcasted_iota(jnp.int32, (tr, 1), 0)
    t = row.astype(F32) * (1.0 / (SEQ - 1))
    decay = jnp.exp(-t * dl_ref[...]) + DECAY_SHIFT
    c = HYENA_WIDTH
    o_ref[:, :c] = (h[:, :c] * decay).astype(BF16)
    o_ref[:, c:] = jnp.where(row == 0, 0.0, h[:, c:] * decay).astype(BF16)


def _filters(zpad, w1pad, b1, freq, w2, b2, w3, deltas, tr=256):
    L = SEQ
    c2 = 2 * HYENA_WIDTH
    full = lambda shape: pl.BlockSpec(shape, lambda i: (0, 0))
    return pl.pallas_call(
        functools.partial(_filter_kernel, tr=tr),
        grid=(L // tr,),
        in_specs=[pl.BlockSpec((tr, LANE), lambda i: (i, 0)),
                  full((LANE, FILTER_HIDDEN)), full((1, FILTER_HIDDEN)), full((1, FILTER_HIDDEN)),
                  full((FILTER_HIDDEN, FILTER_HIDDEN)), full((1, FILTER_HIDDEN)),
                  full((FILTER_HIDDEN, c2)), full((1, HYENA_WIDTH))],
        out_specs=pl.BlockSpec((tr, c2), lambda i: (i, 0)),
        out_shape=jax.ShapeDtypeStruct((L, c2), BF16),
        compiler_params=_params(("parallel",), 48),
        name="hyena_filters",
    )(zpad, w1pad, b1, freq, w2, b2, w3, deltas)


def _kspec_kernel(f_ref, h_ref, o_ref, fb_ref, g_ref):
    c = HYENA_WIDTH
    t = DFT_TILE
    L = SEQ
    fb = f_ref[0].astype(BF16)
    fb_ref[0] = fb
    p = jnp.dot(fb, h_ref[...], preferred_element_type=F32)
    row = pl.program_id(0) * t + lax.broadcasted_iota(jnp.int32, (t, 1), 0)
    o_ref[0] = p[:t, :c] + p[:t, c:]
    o_ref[1] = jnp.where(row == 0, p[t:, :c] + p[t:, c:], p[t:, :c] - p[t:, c:])
    col = lax.broadcasted_iota(jnp.int32, (1, L), 1)
    sign = (1 - 2 * (row & 1)).astype(F32)
    gc = f_ref[0, :t, :] * jnp.where(col == 0, 1.0 / FFT_N, 2.0 / FFT_N)
    gs = jnp.where(col == 0, sign * (1.0 / FFT_N),
                   jnp.where(row == 0, 0.0, f_ref[0, t:, :] * (2.0 / FFT_N)))
    g_ref[:, :L] = gc.astype(BF16)
    g_ref[:, L:] = gs.astype(BF16)


def _kspec(table, hcat):
    L = SEQ
    c = HYENA_WIDTH
    t = DFT_TILE
    return pl.pallas_call(
        _kspec_kernel,
        grid=(L // t,),
        in_specs=[pl.BlockSpec((1, 2 * t, L), lambda i: (i, 0, 0)),
                  pl.BlockSpec((L, 2 * c), lambda i: (0, 0), pipeline_mode=pl.Buffered(1))],
        out_specs=[pl.BlockSpec((2, t, c), lambda i: (0, i, 0)),
                   pl.BlockSpec((1, 2 * t, L), lambda i: (i, 0, 0)),
                   pl.BlockSpec((t, 2 * L), lambda i: (i, 0))],
        out_shape=[jax.ShapeDtypeStruct((2, L, c), F32),
                   jax.ShapeDtypeStruct(table.shape, BF16),
                   jax.ShapeDtypeStruct((L, 2 * L), BF16)],
        compiler_params=_params(("arbitrary",), 56),
        name="hyena_kspec",
    )(table, hcat)


def _fwd_kernel(f_ref, vg_ref, k_ref, y_ref):
    t = DFT_TILE
    for bb in range(vg_ref.shape[0]):
        vg = vg_ref[bb]
        for r0 in range(0, t, ROW_CHUNK):
            xr = jnp.dot(f_ref[0, pl.ds(r0, ROW_CHUNK), :], vg, preferred_element_type=F32)
            xi = jnp.dot(f_ref[0, pl.ds(t + r0, ROW_CHUNK), :], vg, preferred_element_type=F32)
            kr = k_ref[0, pl.ds(r0, ROW_CHUNK), :]
            ki = k_ref[1, pl.ds(r0, ROW_CHUNK), :]
            row = pl.program_id(0) * t + r0 + lax.broadcasted_iota(jnp.int32, (ROW_CHUNK, 1), 0)
            first = row == 0
            yr = jnp.where(first, xr * kr, xr * kr - xi * ki)
            yi = jnp.where(first, xi * ki, xr * ki + xi * kr)
            y_ref[bb, 0, pl.ds(r0, ROW_CHUNK), :] = yr.astype(BF16)
            y_ref[bb, 1, pl.ds(r0, ROW_CHUNK), :] = yi.astype(BF16)


def _fwd_dft(table_bf16, vg, kspec, nb=2):
    b, L, c = vg.shape
    t = DFT_TILE
    return pl.pallas_call(
        _fwd_kernel,
        grid=(L // t, b // nb),
        in_specs=[pl.BlockSpec((1, 2 * t, L), lambda j, bi: (j, 0, 0)),
                  pl.BlockSpec((nb, L, c), lambda j, bi: (bi, 0, 0)),
                  pl.BlockSpec((2, t, c), lambda j, bi: (0, j, 0))],
        out_specs=pl.BlockSpec((nb, 2, t, c), lambda j, bi: (bi, 0, j, 0)),
        out_shape=jax.ShapeDtypeStruct((b, 2, L, c), BF16),
        compiler_params=_params(("arbitrary", "arbitrary"), 56),
        name="hyena_fwd_dft",
    )(table_bf16, vg, kspec)


def _inv_kernel(g_ref, y_ref, vg_ref, x1_ref, d_ref, gain_ref, o_ref):
    for r0 in range(0, o_ref.shape[1], ROW_CHUNK):
        rows = pl.ds(r0, ROW_CHUNK)
        y = jnp.dot(g_ref[rows, :], y_ref[0], preferred_element_type=F32)
        vg = vg_ref[0, rows, :].astype(F32)
        y = (y + vg * d_ref[...]) * x1_ref[0, rows, :].astype(F32)
        ms = jnp.mean(y * y, axis=-1, keepdims=True)
        o_ref[0, rows, :] = (y * lax.rsqrt(ms + EPS) * gain_ref[...]).astype(BF16)


def _inv_dft(inv_weights, yspec, vg, x1, d_skip, gain, tt=1024):
    b, L, c = vg.shape
    return pl.pallas_call(
        _inv_kernel,
        grid=(L // tt, b),
        in_specs=[pl.BlockSpec((tt, 2 * L), lambda j, bi: (j, 0)),
                  pl.BlockSpec((1, 2 * L, c), lambda j, bi: (bi, 0, 0)),
                  pl.BlockSpec((1, tt, c), lambda j, bi: (bi, j, 0)),
                  pl.BlockSpec((1, tt, c), lambda j, bi: (bi, j, 0)),
                  pl.BlockSpec((1, c), lambda j, bi: (0, 0)),
                  pl.BlockSpec((1, c), lambda j, bi: (0, 0))],
        out_specs=pl.BlockSpec((1, tt, c), lambda j, bi: (bi, j, 0)),
        out_shape=jax.ShapeDtypeStruct((b, L, c), BF16),
        compiler_params=_params(("arbitrary", "arbitrary"), 56),
        name="hyena_inv_dft",
    )(inv_weights, yspec, vg, x1, d_skip, gain)


def _layer_norm(y, g, b):
    mu = jnp.mean(y, axis=-1, keepdims=True)
    yc = y - mu
    var = jnp.mean(yc * yc, axis=-1, keepdims=True)
    return yc * lax.rsqrt(var + EPS) * g + b


def _outproj_kernel(att_ref, hy_ref, w_ref, x_ref, g_ref, b_ref, o_ref):
    a = ATTN_WIDTH
    for r0 in range(0, o_ref.shape[0], ROW_CHUNK):
        rows = pl.ds(r0, ROW_CHUNK)
        mix = jnp.dot(att_ref[rows, :], w_ref[:a, :], preferred_element_type=F32)
        mix = mix + jnp.dot(hy_ref[rows, :], w_ref[a:, :], preferred_element_type=F32)
        y = ALPHA * x_ref[rows, :] + mix
        o_ref[rows, :] = _layer_norm(y, g_ref[...], b_ref[...])


def _outproj(att2d, hy2d, w_bf16, x2d, g, b, tm=1024):
    m, d = x2d.shape
    return pl.pallas_call(
        _outproj_kernel,
        grid=(m // tm,),
        in_specs=[pl.BlockSpec((tm, ATTN_WIDTH), lambda i: (i, 0)),
                  pl.BlockSpec((tm, HYENA_WIDTH), lambda i: (i, 0)),
                  pl.BlockSpec((d, d), lambda i: (0, 0), pipeline_mode=pl.Buffered(1)),
                  pl.BlockSpec((tm, d), lambda i: (i, 0)),
                  pl.BlockSpec((1, d), lambda i: (0, 0)),
                  pl.BlockSpec((1, d), lambda i: (0, 0))],
        out_specs=pl.BlockSpec((tm, d), lambda i: (i, 0)),
        out_shape=jax.ShapeDtypeStruct((m, d), F32),
        compiler_params=_params(("parallel",), 56),
        name="outproj_ln",
    )(att2d, hy2d, w_bf16, x2d, g, b)


def _ffn_kernel(x_ref, w1_ref, w2_ref, g_ref, b_ref, o_ref, xb_ref):
    f = pl.program_id(1)
    last = pl.num_programs(1) - 1
    tm = o_ref.shape[0]

    def mlp(xb, base):
        h = jnp.dot(xb, w1_ref[...], preferred_element_type=F32)
        h = jnp.square(jnp.maximum(h, 0.0)).astype(BF16)
        return base + jnp.dot(h, w2_ref[...], preferred_element_type=F32)

    @pl.when(f == 0)
    def _():
        for r0 in range(0, tm, ROW_CHUNK):
            rows = pl.ds(r0, ROW_CHUNK)
            x = x_ref[rows, :]
            xb = x.astype(BF16)
            xb_ref[rows, :] = xb
            o_ref[rows, :] = mlp(xb, ALPHA * x)

    @pl.when(jnp.logical_and(f > 0, f < last))
    def _():
        for r0 in range(0, tm, 2 * ROW_CHUNK):
            rows = pl.ds(r0, 2 * ROW_CHUNK)
            o_ref[rows, :] = mlp(xb_ref[rows, :], o_ref[rows, :])

    @pl.when(f == last)
    def _():
        for r0 in range(0, tm, ROW_CHUNK):
            rows = pl.ds(r0, ROW_CHUNK)
            y = mlp(xb_ref[rows, :], o_ref[rows, :])
            o_ref[rows, :] = _layer_norm(y, g_ref[...], b_ref[...])


def _ffn(x2d, w1_bf16, w2_bf16, g, b, tm=1024, tf=FFN_TILE):
    m, d = x2d.shape
    dff = w1_bf16.shape[1]
    return pl.pallas_call(
        _ffn_kernel,
        grid=(m // tm, dff // tf),
        in_specs=[pl.BlockSpec((tm, d), lambda i, j: (i, 0)),
                  pl.BlockSpec((d, tf), lambda i, j: (0, j)),
                  pl.BlockSpec((tf, d), lambda i, j: (j, 0)),
                  pl.BlockSpec((1, d), lambda i, j: (0, 0)),
                  pl.BlockSpec((1, d), lambda i, j: (0, 0))],
        out_specs=pl.BlockSpec((tm, d), lambda i, j: (i, 0)),
        out_shape=jax.ShapeDtypeStruct((m, d), F32),
        scratch_shapes=[pltpu.VMEM((tm, d), BF16)],
        compiler_params=_params(("parallel", "arbitrary"), 58),
        name="ffn_ln",
    )(x2d, w1_bf16, w2_bf16, g, b)


def kernel(x, w_in, lambda_q1, lambda_k1, lambda_q2, lambda_k2, subln_g, conv_w, conv_b, filt_w1, filt_b1,
           filt_freq, filt_w2, filt_b2, filt_w3, hyena_skip, hyena_gain, w_out, ln1_g, ln1_b, w_ff1, w_ff2,
           ln2_g, ln2_b):
    B, S, D = x.shape
    assert (B, S, D) == (BATCH, SEQ, D_MODEL)
    zpad_np, deltas_np = _filter_tables()
    dft_tab = jnp.asarray(_dft_table())
    zpad = jnp.asarray(zpad_np)
    deltas = jnp.asarray(deltas_np)
    row = lambda v: v.astype(F32).reshape(1, -1)

    x2d = x.reshape(B * S, D)
    for l in range(DEPTH):
        lam_init = 0.8 - 0.6 * math.exp(-0.3 * l)
        lam = (jnp.exp(jnp.sum(lambda_q1[l].astype(F32) * lambda_k1[l].astype(F32)))
               - jnp.exp(jnp.sum(lambda_q2[l].astype(F32) * lambda_k2[l].astype(F32)))
               + lam_init).reshape(1)

        col_scale = np.ones((1, IN_COLS), np.float32)
        col_scale[:, :ATTN_WIDTH] = Q_PRESCALE
        proj, vg, x1, w_out_b, w_ff1_b, w_ff2_b = _inproj(
            x2d, w_in[l].astype(BF16), jnp.asarray(col_scale), conv_w[l].astype(F32), row(conv_b[l]),
            [w_out[l].astype(F32), w_ff1[l].astype(F32), w_ff2[l].astype(F32)])
        proj3 = proj.reshape(B, S, 3 * ATTN_WIDTH)
        vg = vg.reshape(B, S, HYENA_WIDTH)
        x1 = x1.reshape(B, S, HYENA_WIDTH)

        att = _attention(proj3, lam, row(subln_g[l]), lam_init)

        w1pad = jnp.zeros((LANE, FILTER_HIDDEN), F32).at[:FILTER_EMB].set(filt_w1[l].astype(F32))
        hcat = _filters(zpad, w1pad, row(filt_b1[l]), row(filt_freq[l]), filt_w2[l].astype(F32),
                        row(filt_b2[l]), filt_w3[l].astype(F32), deltas)
        kspec, dft_tab_b, inv_weights = _kspec(dft_tab, hcat)
        yspec = _fwd_dft(dft_tab_b, vg, kspec)
        hy = _inv_dft(inv_weights, yspec.reshape(B, 2 * S, HYENA_WIDTH), vg, x1, row(hyena_skip[l]),
                      row(hyena_gain[l]))

        x2d = _outproj(att.reshape(B * S, ATTN_WIDTH), hy.reshape(B * S, HYENA_WIDTH), w_out_b,
                       x2d, row(ln1_g[l]), row(ln1_b[l]))
        x2d = _ffn(x2d, w_ff1_b, w_ff2_b, row(ln2_g[l]), row(ln2_b[l]))
    return x2d.reshape(B, S, D)
```

```python
import functools
import math

import numpy as np
import jax
import jax.numpy as jnp
from jax import lax
from jax.experimental import pallas as pl
from jax.experimental.pallas import tpu as pltpu

D_MODEL = 2048
BATCH = 8
SEQ = 2048
DEPTH = 1
ATTN_WIDTH = D_MODEL // 2
HYENA_WIDTH = D_MODEL - ATTN_WIDTH
N_HEADS = 8
HEAD_DIM = ATTN_WIDTH // N_HEADS // 2
V_HEAD_DIM = 2 * HEAD_DIM
SHORT_CONV = 3
FILTER_EMB = 33
FILTER_HIDDEN = 64
DECAY_FAST = 0.3
DECAY_SLOW = 1.5
DECAY_TARGET = 1e-2
DECAY_SHIFT = 0.0
D_FF = 4 * D_MODEL
ALPHA = (2.0 * DEPTH) ** 0.25
EPS = 1e-5
IN_COLS = 3 * ATTN_WIDTH + 3 * HYENA_WIDTH
FFT_N = 2 * SEQ
LANE = 128
MXU_DIM = 256
DFT_TILE = 512
FFN_TILE = 1024
ROW_CHUNK = 256
ATTN_TILE = 512
ATTN_KEYS = 512
HALO = 16
ATTN_COL_TILES = 2
PV_ROWS = V_HEAD_DIM + 16
LOG2E = math.log2(math.e)
Q_PRESCALE = HEAD_DIM ** -0.5 * LOG2E

BF16 = jnp.bfloat16
F32 = jnp.float32

_MIB = 1024 * 1024


def _params(semantics, vmem_mib):
    return pltpu.CompilerParams(dimension_semantics=semantics, vmem_limit_bytes=vmem_mib * _MIB)


@functools.lru_cache(maxsize=None)
def _dft_table():
    k = np.arange(SEQ, dtype=np.int64)
    phase = (k[:, None] * k[None, :]) % FFT_N
    ang = (2.0 * np.pi / FFT_N) * phase
    c = np.cos(ang)
    s = -np.sin(ang)
    s[0, :] = np.where(k % 2 == 0, 1.0, -1.0)
    nt = SEQ // DFT_TILE
    tiled = np.concatenate([c.reshape(nt, DFT_TILE, SEQ), s.reshape(nt, DFT_TILE, SEQ)], axis=1)
    return tiled.astype(np.float32)


@functools.lru_cache(maxsize=None)
def _filter_tables():
    L = SEQ
    t = np.linspace(0.0, 1.0, L)[:, None]
    bands = (FILTER_EMB - 1) // 2
    w = 2.0 * np.pi * np.arange(L, dtype=np.float64)[:, None] / L
    f = np.linspace(1e-4, bands - 1, bands)[None, :]
    z = np.concatenate([t, np.cos(f * w), -np.sin(f * w)], axis=-1)
    zpad = np.zeros((L, LANE), np.float32)
    zpad[:, :FILTER_EMB] = z
    max_decay = math.log(DECAY_TARGET) / DECAY_FAST
    min_decay = math.log(DECAY_TARGET) / DECAY_SLOW
    deltas = np.abs(np.linspace(min_decay, max_decay, HYENA_WIDTH))[None, :].astype(np.float32)
    return zpad, deltas


def _inproj_kernel(x_ref, xp_ref, xn_ref, wa_ref, wb_ref, wc_ref, sa_ref, sb_ref, sc_ref,
                   cwa_ref, cwb_ref, cwc_ref, cba_ref, cbb_ref, cbc_ref, ea_ref, eb_ref, ec_ref,
                   o_ref, vg_ref, x1_ref, ea_out, eb_out, ec_out, xe_ref):
    i = pl.program_id(0)
    j = pl.program_id(1)
    tm = x_ref.shape[0]
    tc = wa_ref.shape[1]
    tiles_per_seq = SEQ // tm
    ea_out[...] = ea_ref[...].astype(BF16)
    eb_out[...] = eb_ref[...].astype(BF16)
    ec_out[...] = ec_ref[...].astype(BF16)

    @pl.when(j == 0)
    def _():
        pos = lax.rem(i, tiles_per_seq)
        xe_ref[:HALO, :] = jnp.where(pos == 0, 0.0, xp_ref[...]).astype(BF16)
        xe_ref[HALO:HALO + tm, :] = x_ref[...].astype(BF16)
        xe_ref[HALO + tm:, :] = jnp.where(pos == tiles_per_seq - 1, 0.0, xn_ref[...]).astype(BF16)

    @pl.when(j < ATTN_COL_TILES)
    def _():
        xb = xe_ref[HALO:HALO + tm, :]
        for p, (w_ref, s_ref) in enumerate(((wa_ref, sa_ref), (wb_ref, sb_ref), (wc_ref, sc_ref))):
            acc = jnp.dot(xb, w_ref[...], preferred_element_type=F32)
            o_ref[:, p * tc:(p + 1) * tc] = (acc * s_ref[...]).astype(BF16)

    @pl.when(j >= ATTN_COL_TILES)
    def _():
        xe = xe_ref[...]
        rows = xe.shape[0]
        for c0 in range(0, tc, MXU_DIM):
            cols = slice(c0, c0 + MXU_DIM)
            z = []
            for w_ref, cw_ref, cb_ref in ((wa_ref, cwa_ref, cba_ref), (wb_ref, cwb_ref, cbb_ref),
                                          (wc_ref, cwc_ref, cbc_ref)):
                u = jnp.dot(xe, w_ref[:, cols], preferred_element_type=F32)
                cw = cw_ref[:, cols]
                zz = (cb_ref[:, cols] + pltpu.roll(u, 1, 0) * cw[0:1] + u * cw[1:2]
                      + pltpu.roll(u, rows - 1, 0) * cw[2:3])
                z.append(zz[HALO:HALO + tm])
            x1, x2, v = z
            vg_ref[:, cols] = (v * x2).astype(BF16)
            x1_ref[:, cols] = x1.astype(BF16)


def _inproj(x2d, w_bf16, col_scale, conv_w, conv_b, side_weights, tm=1024, tc=512):
    m, k = x2d.shape
    attn_cols = 3 * ATTN_WIDTH
    na = ATTN_COL_TILES
    assert attn_cols == na * 3 * tc
    nh_tiles = HYENA_WIDTH // tc
    nj = na + nh_tiles
    steps = (m // tm) * nj
    halo_per_tile = tm // HALO

    def col_block(p):
        return lambda i, j: (0, jnp.where(j < na, 3 * j + p, 3 * na + nh_tiles * p + (j - na)))

    def conv_block(p):
        return lambda i, j: (0, nh_tiles * p + jnp.maximum(j - na, 0))

    def slab(w):
        return pl.BlockSpec((w.shape[0] // steps, w.shape[1]), lambda i, j: (i * nj + j, 0))

    side_specs = [slab(w) for w in side_weights]
    hy_out = pl.BlockSpec((tm, tc), lambda i, j: (i, jnp.maximum(j - na, 0)))
    return pl.pallas_call(
        _inproj_kernel,
        grid=(m // tm, nj),
        in_specs=[pl.BlockSpec((tm, k), lambda i, j: (i, 0)),
                  pl.BlockSpec((HALO, k), lambda i, j: (jnp.maximum(i * halo_per_tile - 1, 0), 0)),
                  pl.BlockSpec((HALO, k), lambda i, j: (jnp.minimum((i + 1) * halo_per_tile, m // HALO - 1), 0))]
                 + [pl.BlockSpec((k, tc), col_block(p)) for p in range(3)]
                 + [pl.BlockSpec((1, tc), col_block(p)) for p in range(3)]
                 + [pl.BlockSpec((SHORT_CONV, tc), conv_block(p)) for p in range(3)]
                 + [pl.BlockSpec((1, tc), conv_block(p)) for p in range(3)]
                 + side_specs,
        out_specs=[pl.BlockSpec((tm, 3 * tc), lambda i, j: (i, jnp.minimum(j, na - 1))), hy_out, hy_out] + side_specs,
        out_shape=[jax.ShapeDtypeStruct((m, attn_cols), BF16),
                   jax.ShapeDtypeStruct((m, HYENA_WIDTH), BF16),
                   jax.ShapeDtypeStruct((m, HYENA_WIDTH), BF16)]
                  + [jax.ShapeDtypeStruct(w.shape, BF16) for w in side_weights],
        scratch_shapes=[pltpu.VMEM((tm + 2 * HALO, k), BF16)],
        compiler_params=_params(("arbitrary", "arbitrary"), 58),
        name="inproj",
    )(x2d, x2d, x2d, w_bf16, w_bf16, w_bf16, col_scale, col_scale, col_scale,
      conv_w, conv_w, conv_w, conv_b, conv_b, conv_b, *side_weights)


@functools.lru_cache(maxsize=None)
def _alibi_tables():
    pos = np.arange(SEQ)
    hi = ((pos >> 8) << 8).astype(np.float64)
    lo = (pos & 255).astype(np.float64)
    augq = np.zeros((N_HEADS, SEQ, V_HEAD_DIM), np.float64)
    augk = np.zeros((N_HEADS, SEQ, V_HEAD_DIM), np.float64)
    t = ATTN_TILE
    ahead = np.maximum(np.arange(t)[:, None] - np.arange(t)[None, :], 0).astype(np.float64)
    dg = np.zeros((N_HEADS, t, t), np.float64)
    for h in range(N_HEADS):
        c = 2.0 ** (-8.0 * (h + 1) / N_HEADS) * LOG2E
        rest = c
        for p in range(3):
            piece = float(np.float32(rest).astype(BF16).astype(np.float64))
            rest -= piece
            for base in (0, HEAD_DIM):
                augq[h, :, base + 2 * p] = piece
                augq[h, :, base + 2 * p + 1] = piece
                augk[h, :, base + 2 * p] = hi
                augk[h, :, base + 2 * p + 1] = lo
                augq[h, :, base + 6 + 2 * p] = -hi
                augq[h, :, base + 6 + 2 * p + 1] = -lo
                augk[h, :, base + 6 + 2 * p] = piece
                augk[h, :, base + 6 + 2 * p + 1] = piece
        dg[h] = -2.0 * c * ahead
    return augq.astype(BF16), augk.astype(BF16), dg.astype(np.float32)


def _attn_kernel(lam_ref, q_ref, k_ref, v_ref, aq_ref, ak_ref, dg_ref, g_ref, o_ref,
                 kt_ref, sa_ref, sb_ref, ma_ref, mb_ref, vt_ref, ql_ref, acc_ref, *, lam_init, n_items):
    t = ATTN_TILE
    tk = ATTN_KEYS
    nkb = SEQ // tk
    nq = SEQ // t
    diag_blocks = t // tk
    j = pl.program_id(0)
    qb = lax.rem(jnp.minimum(j, n_items - 1), nq)
    lam = lam_ref[0]
    lane = lax.broadcasted_iota(jnp.int32, (1, V_HEAD_DIM), 1)
    first_half = lane < HEAD_DIM
    nt_dims = (((1,), (1,)), ((), ()))

    @pl.when(j == 0)
    def _():
        sb_ref[...] = jnp.zeros_like(sb_ref)
        mb_ref[...] = jnp.zeros_like(mb_ref)

    @pl.when(qb == 0)
    def _():
        k = k_ref[0]
        ak = ak_ref[0]
        kt_ref[0] = jnp.where(first_half, k, ak)
        kt_ref[1] = jnp.where(first_half, ak, k)

    @pl.when(lax.rem(jnp.maximum(j - 1, 0), nq) == 0)
    def _():
        pad_row = lax.broadcasted_iota(jnp.int32, (PV_ROWS - V_HEAD_DIM, 1), 0)
        ones_row = jnp.where(pad_row == 0, 1.0, 0.0).astype(BF16)
        for kb in range(nkb):
            vt_ref[kb, :V_HEAD_DIM, :] = v_ref[0, kb * tk:(kb + 1) * tk, :].astype(F32).T.astype(BF16)
            vt_ref[kb, V_HEAD_DIM:, :] = jnp.broadcast_to(ones_row, (PV_ROWS - V_HEAD_DIM, tk))

    def step(s_new, m_new, s_old, m_old):
        q = q_ref[0]
        aq = aq_ref[0]
        naq = -aq
        for c, keep in enumerate((first_half, jnp.logical_not(first_half))):
            ql_ref[c, 0] = jnp.where(keep, q, aq)
            ql_ref[c, 1] = jnp.where(keep, q, naq)
        col_max = [jnp.max(m_old[c], axis=0, keepdims=True) for c in range(2)]

        def key_block(r):
            first = r == 0
            shared = r < diag_blocks
            start = qb * diag_blocks + r
            wrapped = start >= nkb
            kb = jnp.where(wrapped, start - nkb, start)
            side = 0 if shared else jnp.where(wrapped, 0, 1)
            for c in range(2):
                p = jnp.exp2(s_old[c, r] - col_max[c]).astype(BF16)
                part = jnp.dot(vt_ref[r], p, preferred_element_type=F32)
                acc_ref[c] = part if first else acc_ref[c] + part
                s = lax.dot_general(kt_ref[c, pl.ds(pl.multiple_of(kb * tk, tk), tk), :], ql_ref[c, side], nt_dims,
                                    preferred_element_type=F32)
                if shared:
                    s = s + dg_ref[0, r * tk:(r + 1) * tk, :]
                s_new[c, kb] = s
                bm = s[0:8]
                for i in range(8, tk, 8):
                    bm = jnp.maximum(bm, s[i:i + 8])
                m_new[c] = bm if first else jnp.maximum(m_new[c], bm)

        for r in range(nkb):
            key_block(r)

        outs = [acc_ref[c, :V_HEAD_DIM, :] / acc_ref[c, V_HEAD_DIM:V_HEAD_DIM + 1, :] for c in range(2)]
        a = (outs[0] - lam * outs[1]).T
        ms = jnp.mean(a * a, axis=-1, keepdims=True)
        y = a * lax.rsqrt(ms + EPS) * g_ref[...] * (1.0 - lam_init)
        o_ref[0] = y.astype(BF16)

    @pl.when(lax.rem(j, 2) == 0)
    def _():
        step(sa_ref, ma_ref, sb_ref, mb_ref)

    @pl.when(lax.rem(j, 2) == 1)
    def _():
        step(sb_ref, mb_ref, sa_ref, ma_ref)


def _attention(proj3, lam, subln_g, lam_init):
    b, s, _ = proj3.shape
    nh = N_HEADS
    t = ATTN_TILE
    nq = s // t
    per_head = b * nq
    n_items = nh * per_head
    augq_np, augk_np, dg_np = _alibi_tables()
    kern = functools.partial(_attn_kernel, lam_init=lam_init, n_items=n_items)
    smem = pl.BlockSpec(memory_space=pltpu.SMEM)
    cur = lambda j: jnp.minimum(j, n_items - 1)
    prev = lambda j: jnp.maximum(j - 1, 0)
    head = lambda i: i // per_head
    batch = lambda i: (i % per_head) // nq
    qblk = lambda i: i % nq
    nkb = s // ATTN_KEYS
    score_scratch = pltpu.VMEM((2, nkb, ATTN_KEYS, t), F32)
    max_scratch = pltpu.VMEM((2, 8, t), F32)
    return pl.pallas_call(
        kern,
        grid=(n_items + 1,),
        in_specs=[smem,
                  pl.BlockSpec((1, t, V_HEAD_DIM), lambda j: (batch(cur(j)), qblk(cur(j)), head(cur(j)))),
                  pl.BlockSpec((1, s, V_HEAD_DIM), lambda j: (batch(cur(j)), 0, nh + head(cur(j)))),
                  pl.BlockSpec((1, s, V_HEAD_DIM), lambda j: (batch(prev(j)), 0, 2 * nh + head(prev(j)))),
                  pl.BlockSpec((1, t, V_HEAD_DIM), lambda j: (head(cur(j)), qblk(cur(j)), 0)),
                  pl.BlockSpec((1, s, V_HEAD_DIM), lambda j: (head(cur(j)), 0, 0)),
                  pl.BlockSpec((1, t, t), lambda j: (head(cur(j)), 0, 0), pipeline_mode=pl.Buffered(1)),
                  pl.BlockSpec((1, V_HEAD_DIM), lambda j: (0, 0))],
        out_specs=pl.BlockSpec((1, t, V_HEAD_DIM), lambda j: (batch(prev(j)), qblk(prev(j)), head(prev(j)))),
        out_shape=jax.ShapeDtypeStruct((b, s, ATTN_WIDTH), BF16),
        scratch_shapes=[pltpu.VMEM((2, s, V_HEAD_DIM), BF16), score_scratch, score_scratch,
                        max_scratch, max_scratch, pltpu.VMEM((nkb, PV_ROWS, ATTN_KEYS), BF16),
                        pltpu.VMEM((2, 2, t, V_HEAD_DIM), BF16), pltpu.VMEM((2, PV_ROWS, t), F32)],
        compiler_params=_params(("arbitrary",), 56),
        name="diff_attn",
    )(lam, proj3, proj3, proj3, jnp.asarray(augq_np), jnp.asarray(augk_np), jnp.asarray(dg_np), subln_g)


def _filter_kernel(z_ref, w1_ref, b1_ref, fr_ref, w2_ref, b2_ref, w3_ref, dl_ref, win_ref, o_ref, win_out, *, tr):
    win_out[...] = win_ref[...].astype(BF16)
    hi = lax.Precision.HIGHEST
    fr = fr_ref[...]
    h = jnp.sin(fr * (jnp.dot(z_ref[...], w1_ref[...], precision=hi, preferred_element_type=F32) + b1_ref[...]))
    h = jnp.sin(fr * (jnp.dot(h, w2_ref[...], precision=hi, preferred_element_type=F32) + b2_ref[...]))
    h = jnp.dot(h.astype(BF16), w3_ref[...].astype(BF16), preferred_element_type=F32)
    row = pl.program_id(0) * tr + lax.broadcasted_iota(jnp.int32, (tr, 1), 0)
    t = row.astype(F32) * (1.0 / (SEQ - 1))
    decay = jnp.exp(-t * dl_ref[...]) + DECAY_SHIFT
    c = HYENA_WIDTH
    o_ref[:, :c] = (h[:, :c] * decay).astype(BF16)
    o_ref[:, c:] = jnp.where(row == 0, 0.0, h[:, c:] * decay).astype(BF16)


def _filters(zpad, w1pad, b1, freq, w2, b2, w3, deltas, w_in, tr=256):
    L = SEQ
    c2 = 2 * HYENA_WIDTH
    steps = L // tr
    full = lambda shape: pl.BlockSpec(shape, lambda i: (0, 0))
    win_spec = pl.BlockSpec((w_in.shape[0] // steps, w_in.shape[1]), lambda i: (i, 0))
    return pl.pallas_call(
        functools.partial(_filter_kernel, tr=tr),
        grid=(steps,),
        in_specs=[pl.BlockSpec((tr, LANE), lambda i: (i, 0)),
                  full((LANE, FILTER_HIDDEN)), full((1, FILTER_HIDDEN)), full((1, FILTER_HIDDEN)),
                  full((FILTER_HIDDEN, FILTER_HIDDEN)), full((1, FILTER_HIDDEN)),
                  full((FILTER_HIDDEN, c2)), full((1, HYENA_WIDTH)), win_spec],
        out_specs=[pl.BlockSpec((tr, c2), lambda i: (i, 0)), win_spec],
        out_shape=[jax.ShapeDtypeStruct((L, c2), BF16), jax.ShapeDtypeStruct(w_in.shape, BF16)],
        compiler_params=_params(("parallel",), 48),
        name="hyena_filters",
    )(zpad, w1pad, b1, freq, w2, b2, w3, deltas, w_in)


def _kspec_kernel(f_ref, h_ref, o_ref, fb_ref, g_ref):
    c = HYENA_WIDTH
    t = DFT_TILE
    L = SEQ
    fb = f_ref[0].astype(BF16)
    fb_ref[0] = fb
    p = jnp.dot(fb, h_ref[...], preferred_element_type=F32)
    row = pl.program_id(0) * t + lax.broadcasted_iota(jnp.int32, (t, 1), 0)
    o_ref[0] = p[:t, :c] + p[:t, c:]
    o_ref[1] = jnp.where(row == 0, p[t:, :c] + p[t:, c:], p[t:, :c] - p[t:, c:])
    col = lax.broadcasted_iota(jnp.int32, (1, L), 1)
    sign = (1 - 2 * (row & 1)).astype(F32)
    gc = f_ref[0, :t, :] * jnp.where(col == 0, 1.0 / FFT_N, 2.0 / FFT_N)
    gs = jnp.where(col == 0, sign * (1.0 / FFT_N),
                   jnp.where(row == 0, 0.0, f_ref[0, t:, :] * (2.0 / FFT_N)))
    g_ref[:, :L] = gc.astype(BF16)
    g_ref[:, L:] = gs.astype(BF16)


def _kspec(table, hcat):
    L = SEQ
    c = HYENA_WIDTH
    t = DFT_TILE
    return pl.pallas_call(
        _kspec_kernel,
        grid=(L // t,),
        in_specs=[pl.BlockSpec((1, 2 * t, L), lambda i: (i, 0, 0)),
                  pl.BlockSpec((L, 2 * c), lambda i: (0, 0), pipeline_mode=pl.Buffered(1))],
        out_specs=[pl.BlockSpec((2, t, c), lambda i: (0, i, 0)),
                   pl.BlockSpec((1, 2 * t, L), lambda i: (i, 0, 0)),
                   pl.BlockSpec((t, 2 * L), lambda i: (i, 0))],
        out_shape=[jax.ShapeDtypeStruct((2, L, c), F32),
                   jax.ShapeDtypeStruct(table.shape, BF16),
                   jax.ShapeDtypeStruct((L, 2 * L), BF16)],
        compiler_params=_params(("arbitrary",), 56),
        name="hyena_kspec",
    )(table, hcat)


def _fwd_kernel(f_ref, vg_ref, k_ref, y_ref):
    t = DFT_TILE
    for bb in range(vg_ref.shape[0]):
        vg = vg_ref[bb]
        for r0 in range(0, t, ROW_CHUNK):
            xr = jnp.dot(f_ref[0, pl.ds(r0, ROW_CHUNK), :], vg, preferred_element_type=F32)
            xi = jnp.dot(f_ref[0, pl.ds(t + r0, ROW_CHUNK), :], vg, preferred_element_type=F32)
            kr = k_ref[0, pl.ds(r0, ROW_CHUNK), :]
            ki = k_ref[1, pl.ds(r0, ROW_CHUNK), :]
            row = pl.program_id(0) * t + r0 + lax.broadcasted_iota(jnp.int32, (ROW_CHUNK, 1), 0)
            first = row == 0
            yr = jnp.where(first, xr * kr, xr * kr - xi * ki)
            yi = jnp.where(first, xi * ki, xr * ki + xi * kr)
            y_ref[bb, 0, pl.ds(r0, ROW_CHUNK), :] = yr.astype(BF16)
            y_ref[bb, 1, pl.ds(r0, ROW_CHUNK), :] = yi.astype(BF16)


def _fwd_dft(table_bf16, vg, kspec, nb=2):
    b, L, c = vg.shape
    t = DFT_TILE
    return pl.pallas_call(
        _fwd_kernel,
        grid=(L // t, b // nb),
        in_specs=[pl.BlockSpec((1, 2 * t, L), lambda j, bi: (j, 0, 0)),
                  pl.BlockSpec((nb, L, c), lambda j, bi: (bi, 0, 0)),
                  pl.BlockSpec((2, t, c), lambda j, bi: (0, j, 0))],
        out_specs=pl.BlockSpec((nb, 2, t, c), lambda j, bi: (bi, 0, j, 0)),
        out_shape=jax.ShapeDtypeStruct((b, 2, L, c), BF16),
        compiler_params=_params(("arbitrary", "arbitrary"), 56),
        name="hyena_fwd_dft",
    )(table_bf16, vg, kspec)


def _inv_kernel(g_ref, y_ref, vg_ref, x1_ref, d_ref, gain_ref, o_ref):
    for r0 in range(0, o_ref.shape[1], ROW_CHUNK):
        rows = pl.ds(r0, ROW_CHUNK)
        y = jnp.dot(g_ref[rows, :], y_ref[0], preferred_element_type=F32)
        vg = vg_ref[0, rows, :].astype(F32)
        y = (y + vg * d_ref[...]) * x1_ref[0, rows, :].astype(F32)
        ms = jnp.mean(y * y, axis=-1, keepdims=True)
        o_ref[0, rows, :] = (y * lax.rsqrt(ms + EPS) * gain_ref[...]).astype(BF16)


def _inv_dft(inv_weights, yspec, vg, x1, d_skip, gain, tt=1024):
    b, L, c = vg.shape
    return pl.pallas_call(
        _inv_kernel,
        grid=(L // tt, b),
        in_specs=[pl.BlockSpec((tt, 2 * L), lambda j, bi: (j, 0)),
                  pl.BlockSpec((1, 2 * L, c), lambda j, bi: (bi, 0, 0)),
                  pl.BlockSpec((1, tt, c), lambda j, bi: (bi, j, 0)),
                  pl.BlockSpec((1, tt, c), lambda j, bi: (bi, j, 0)),
                  pl.BlockSpec((1, c), lambda j, bi: (0, 0)),
                  pl.BlockSpec((1, c), lambda j, bi: (0, 0))],
        out_specs=pl.BlockSpec((1, tt, c), lambda j, bi: (bi, j, 0)),
        out_shape=jax.ShapeDtypeStruct((b, L, c), BF16),
        compiler_params=_params(("arbitrary", "arbitrary"), 56),
        name="hyena_inv_dft",
    )(inv_weights, yspec, vg, x1, d_skip, gain)


def _layer_norm(y, g, b):
    mu = jnp.mean(y, axis=-1, keepdims=True)
    yc = y - mu
    var = jnp.mean(yc * yc, axis=-1, keepdims=True)
    return yc * lax.rsqrt(var + EPS) * g + b


def _outproj_kernel(att_ref, hy_ref, w_ref, x_ref, g_ref, b_ref, o_ref):
    a = ATTN_WIDTH
    for r0 in range(0, o_ref.shape[0], ROW_CHUNK):
        rows = pl.ds(r0, ROW_CHUNK)
        mix = jnp.dot(att_ref[rows, :], w_ref[:a, :], preferred_element_type=F32)
        mix = mix + jnp.dot(hy_ref[rows, :], w_ref[a:, :], preferred_element_type=F32)
        y = ALPHA * x_ref[rows, :] + mix
        o_ref[rows, :] = _layer_norm(y, g_ref[...], b_ref[...])


def _outproj(att2d, hy2d, w_bf16, x2d, g, b, tm=1024):
    m, d = x2d.shape
    return pl.pallas_call(
        _outproj_kernel,
        grid=(m // tm,),
        in_specs=[pl.BlockSpec((tm, ATTN_WIDTH), lambda i: (i, 0)),
                  pl.BlockSpec((tm, HYENA_WIDTH), lambda i: (i, 0)),
                  pl.BlockSpec((d, d), lambda i: (0, 0), pipeline_mode=pl.Buffered(1)),
                  pl.BlockSpec((tm, d), lambda i: (i, 0)),
                  pl.BlockSpec((1, d), lambda i: (0, 0)),
                  pl.BlockSpec((1, d), lambda i: (0, 0))],
        out_specs=pl.BlockSpec((tm, d), lambda i: (i, 0)),
        out_shape=jax.ShapeDtypeStruct((m, d), F32),
        compiler_params=_params(("parallel",), 56),
        name="outproj_ln",
    )(att2d, hy2d, w_bf16, x2d, g, b)


def _ffn_kernel(x_ref, w1_ref, w2_ref, g_ref, b_ref, o_ref, xb_ref):
    f = pl.program_id(1)
    last = pl.num_programs(1) - 1
    tm = o_ref.shape[0]

    def mlp(xb, base):
        h = jnp.dot(xb, w1_ref[...], preferred_element_type=F32)
        h = jnp.square(jnp.maximum(h, 0.0)).astype(BF16)
        return base + jnp.dot(h, w2_ref[...], preferred_element_type=F32)

    @pl.when(f == 0)
    def _():
        for r0 in range(0, tm, ROW_CHUNK):
            rows = pl.ds(r0, ROW_CHUNK)
            x = x_ref[rows, :]
            xb = x.astype(BF16)
            xb_ref[rows, :] = xb
            o_ref[rows, :] = mlp(xb, ALPHA * x)

    @pl.when(jnp.logical_and(f > 0, f < last))
    def _():
        for r0 in range(0, tm, 2 * ROW_CHUNK):
            rows = pl.ds(r0, 2 * ROW_CHUNK)
            o_ref[rows, :] = mlp(xb_ref[rows, :], o_ref[rows, :])

    @pl.when(f == last)
    def _():
        for r0 in range(0, tm, ROW_CHUNK):
            rows = pl.ds(r0, ROW_CHUNK)
            y = mlp(xb_ref[rows, :], o_ref[rows, :])
            o_ref[rows, :] = _layer_norm(y, g_ref[...], b_ref[...])


def _ffn(x2d, w1_bf16, w2_bf16, g, b, tm=1024, tf=FFN_TILE):
    m, d = x2d.shape
    dff = w1_bf16.shape[1]
    return pl.pallas_call(
        _ffn_kernel,
        grid=(m // tm, dff // tf),
        in_specs=[pl.BlockSpec((tm, d), lambda i, j: (i, 0)),
                  pl.BlockSpec((d, tf), lambda i, j: (0, j)),
                  pl.BlockSpec((tf, d), lambda i, j: (j, 0)),
                  pl.BlockSpec((1, d), lambda i, j: (0, 0)),
                  pl.BlockSpec((1, d), lambda i, j: (0, 0))],
        out_specs=pl.BlockSpec((tm, d), lambda i, j: (i, 0)),
        out_shape=jax.ShapeDtypeStruct((m, d), F32),
        scratch_shapes=[pltpu.VMEM((tm, d), BF16)],
        compiler_params=_params(("parallel", "arbitrary"), 58),
        name="ffn_ln",
    )(x2d, w1_bf16, w2_bf16, g, b)


def kernel(x, w_in, lambda_q1, lambda_k1, lambda_q2, lambda_k2, subln_g, conv_w, conv_b, filt_w1, filt_b1,
           filt_freq, filt_w2, filt_b2, filt_w3, hyena_skip, hyena_gain, w_out, ln1_g, ln1_b, w_ff1, w_ff2,
           ln2_g, ln2_b):
    B, S, D = x.shape
    assert (B, S, D) == (BATCH, SEQ, D_MODEL)
    zpad_np, deltas_np = _filter_tables()
    dft_tab = jnp.asarray(_dft_table())
    zpad = jnp.asarray(zpad_np)
    deltas = jnp.asarray(deltas_np)
    row = lambda v: v.astype(F32).reshape(1, -1)

    x2d = x.reshape(B * S, D)
    for l in range(DEPTH):
        lam_init = 0.8 - 0.6 * math.exp(-0.3 * l)
        lam = (jnp.exp(jnp.sum(lambda_q1[l].astype(F32) * lambda_k1[l].astype(F32)))
               - jnp.exp(jnp.sum(lambda_q2[l].astype(F32) * lambda_k2[l].astype(F32)))
               + lam_init).reshape(1)

        w1pad = jnp.zeros((LANE, FILTER_HIDDEN), F32).at[:FILTER_EMB].set(filt_w1[l].astype(F32))
        hcat, w_in_b = _filters(zpad, w1pad, row(filt_b1[l]), row(filt_freq[l]), filt_w2[l].astype(F32),
                                row(filt_b2[l]), filt_w3[l].astype(F32), deltas, w_in[l].astype(F32))

        col_scale = np.ones((1, IN_COLS), np.float32)
        col_scale[:, :ATTN_WIDTH] = Q_PRESCALE
        proj, vg, x1, w_out_b, w_ff1_b, w_ff2_b = _inproj(
            x2d, w_in_b, jnp.asarray(col_scale), conv_w[l].astype(F32), row(conv_b[l]),
            [w_out[l].astype(F32), w_ff1[l].astype(F32), w_ff2[l].astype(F32)])
        proj3 = proj.reshape(B, S, 3 * ATTN_WIDTH)
        vg = vg.reshape(B, S, HYENA_WIDTH)
        x1 = x1.reshape(B, S, HYENA_WIDTH)

        att = _attention(proj3, lam, row(subln_g[l]), lam_init)

        kspec, dft_tab_b, inv_weights = _kspec(dft_tab, hcat)
        yspec = _fwd_dft(dft_tab_b, vg, kspec)
        hy = _inv_dft(inv_weights, yspec.reshape(B, 2 * S, HYENA_WIDTH), vg, x1, row(hyena_skip[l]),
                      row(hyena_gain[l]))

        x2d = _outproj(att.reshape(B * S, ATTN_WIDTH), hy.reshape(B * S, HYENA_WIDTH), w_out_b,
                       x2d, row(ln1_g[l]), row(ln1_b[l]))
        x2d = _ffn(x2d, w_ff1_b, w_ff2_b, row(ln2_g[l]), row(ln2_b[l]))
    return x2d.reshape(B, S, D)
```
